```python
import math
import jax, jax.numpy as jnp
from jax import lax
import numpy as np

D_MODEL = 1024
BATCH = 2
SEQ = 8192
DEPTH = 2

ATTN_HEADS = 8
ATTN_KV_HEADS = 2
ATTN_HEAD_DIM = D_MODEL // ATTN_HEADS
ATTN_WIDTH = ATTN_HEADS * ATTN_HEAD_DIM
ATTN_KV_WIDTH = ATTN_KV_HEADS * ATTN_HEAD_DIM
ATTN_ROT_DIM = ATTN_HEAD_DIM // 4
IDX_HEADS = 8
IDX_DIM = 64
IDX_ROT_DIM = IDX_DIM // 4
INDEX_TOPK = 256
Q_BLOCK = 128
ROPE_THETA = 500000.0
GDN_HEADS = 8
GDN_DK = 128
GDN_DV = 128
GDN_QK_WIDTH = GDN_HEADS * GDN_DK
GDN_WIDTH = GDN_HEADS * GDN_DV
GDN_QKV_WIDTH = 2 * GDN_QK_WIDTH + GDN_WIDTH
CONV_K = 4
CHUNK = 64
N_BRANCH = 2
NORM_EPS = 1e-6

IN_SPLITS = [
    ATTN_WIDTH,
    ATTN_KV_WIDTH,
    ATTN_KV_WIDTH,
    ATTN_WIDTH,
    IDX_HEADS * IDX_DIM,
    IDX_DIM,
    IDX_HEADS,
    GDN_QKV_WIDTH,
    GDN_WIDTH,
    GDN_HEADS,
    GDN_HEADS,
    N_BRANCH * D_MODEL,
]
IN_COLS = sum(IN_SPLITS)
IN_SPLIT_IDX = [sum(IN_SPLITS[:i + 1]) for i in range(len(IN_SPLITS) - 1)]

kernel_name = "hybrid_dsa_gated_deltanet_parallel"


def rms_norm(x, gain):
    xf = x.astype(jnp.float32)
    y = xf * lax.rsqrt(jnp.mean(xf * xf, axis=-1, keepdims=True) + NORM_EPS)
    return (y * gain.astype(jnp.float32)).astype(x.dtype)


def l2_normalize(x):
    xf = x.astype(jnp.float32)
    return xf * lax.rsqrt(jnp.sum(xf * xf, axis=-1, keepdims=True) + NORM_EPS)


def rope_tables(positions, rot_dim):
    inv_freq = ROPE_THETA ** (-jnp.arange(0, rot_dim, 2, dtype=jnp.float32) / rot_dim)
    ang = positions.astype(jnp.float32)[..., None] * inv_freq
    return jnp.cos(ang)[:, :, None, :], jnp.sin(ang)[:, :, None, :]


def apply_partial_rope(x, cos, sin):
    half = cos.shape[-1]
    rot = 2 * half
    x1, x2, xp = x[..., :half], x[..., half:rot], x[..., rot:]
    c, s = cos.astype(x.dtype), sin.astype(x.dtype)
    return jnp.concatenate([x1 * c - x2 * s, x2 * c + x1 * s, xp], axis=-1)


def dsa_attention(q, k, v, q_idx, k_idx, w_idx):
    B, S = q.shape[0], q.shape[1]
    topk = min(INDEX_TOPK, S // 4)
    nb = S // Q_BLOCK
    group = ATTN_HEADS // ATTN_KV_HEADS
    scale = ATTN_HEAD_DIM ** -0.5
    key_pos = jnp.arange(S, dtype=jnp.int32)

    def blocks(t):
        return t.reshape(B, nb, Q_BLOCK, *t.shape[2:]).swapaxes(0, 1)

    def one_block(args):
        qb, qib, wb, tb = args
        raw = jnp.einsum('bqhd,bsd->bqhs', qib, k_idx)
        iscore = jnp.einsum('bqhs,bqh->bqs', jax.nn.relu(raw), wb).astype(jnp.float32)
        causal = key_pos[None, :] <= tb[:, None]
        iscore = jnp.where(causal[None], iscore, -jnp.inf)
        _, sel = lax.top_k(iscore, topk)
        valid = sel <= tb[None, :, None]
        k_sel = jax.vmap(lambda kb, ib: kb[ib])(k, sel)
        v_sel = jax.vmap(lambda vb, ib: vb[ib])(v, sel)
        qg = qb.reshape(B, Q_BLOCK, ATTN_KV_HEADS, group, ATTN_HEAD_DIM)
        logits = jnp.einsum('bqhgd,bqkhd->bqhgk', qg, k_sel).astype(jnp.float32) * scale
        logits = jnp.where(valid[:, :, None, None, :], logits, -jnp.inf)
        p = jax.nn.softmax(logits, axis=-1).astype(v.dtype)
        o = jnp.einsum('bqhgk,bqkhd->bqhgd', p, v_sel)
        return o.reshape(B, Q_BLOCK, ATTN_WIDTH)

    out = lax.map(one_block, (blocks(q), blocks(q_idx), blocks(w_idx),
                              key_pos.reshape(nb, Q_BLOCK)))
    return out.swapaxes(0, 1).reshape(B, S, ATTN_WIDTH)


def causal_depthwise_conv(x, w):
    C = x.shape[-1]
    return lax.conv_general_dilated(
        x, w[:, None, :].astype(x.dtype), window_strides=(1,),
        padding=[(CONV_K - 1, 0)], dimension_numbers=('NWC', 'WIO', 'NWC'),
        feature_group_count=C)


def chunked_gated_delta_rule(q, k, v, beta, g):
    B, S, H, dk = q.shape
    dv = v.shape[-1]
    nc = S // CHUNK

    def c4(t):
        return t.reshape(B, nc, CHUNK, H, t.shape[-1]).transpose(0, 3, 1, 2, 4)

    def c3(t):
        return t.reshape(B, nc, CHUNK, H).transpose(0, 3, 1, 2)

    q, k, v = c4(q) * (dk ** -0.5), c4(k), c4(v)
    beta, gc = c3(beta), jnp.cumsum(c3(g), axis=-1)
    incl = jnp.tril(jnp.ones((CHUNK, CHUNK), dtype=bool))
    strict = jnp.tril(jnp.ones((CHUNK, CHUNK), dtype=bool), k=-1)
    decay = jnp.exp(jnp.where(incl, gc[..., :, None] - gc[..., None, :], -jnp.inf))
    kb = k * beta[..., None]
    lower = jnp.where(strict, jnp.einsum('bhnid,bhnjd->bhnij', kb, k) * decay, 0.0)
    tmat = jnp.eye(CHUNK, dtype=jnp.float32) + lower
    u = lax.linalg.triangular_solve(tmat, v * beta[..., None], left_side=True,
                                    lower=True, unit_diagonal=True)
    w = lax.linalg.triangular_solve(tmat, kb * jnp.exp(gc)[..., None], left_side=True,
                                    lower=True, unit_diagonal=True)
    qk = jnp.einsum('bhnid,bhnjd->bhnij', q, k) * decay
    q_dec = q * jnp.exp(gc)[..., None]
    k_dec = k * jnp.exp(gc[..., -1:] - gc)[..., None]
    g_tot = jnp.exp(gc[..., -1])
    xs = (jnp.moveaxis(qk, 2, 0), jnp.moveaxis(q_dec, 2, 0), jnp.moveaxis(k_dec, 2, 0),
          jnp.moveaxis(u, 2, 0), jnp.moveaxis(w, 2, 0), jnp.moveaxis(g_tot, 2, 0))

    def step(state, xc):
        qk_c, qd_c, kd_c, u_c, w_c, gt_c = xc
        v_new = u_c - jnp.einsum('bhcd,bhde->bhce', w_c, state)
        o_c = (jnp.einsum('bhcd,bhde->bhce', qd_c, state)
               + jnp.einsum('bhij,bhje->bhie', qk_c, v_new))
        state = state * gt_c[..., None, None] + jnp.einsum('bhcd,bhce->bhde', kd_c, v_new)
        return state, o_c

    state0 = jnp.zeros((B, H, dk, dv), dtype=jnp.float32)
    _, o = lax.scan(step, state0, xs)
    return o.transpose(1, 0, 3, 2, 4).reshape(B, S, H, dv)


def setup_inputs(seed: int = 0) -> dict:
    key = jax.random.key(seed)
    ks = jax.random.split(key, 12)
    x = jax.random.normal(ks[0], (BATCH, SEQ, D_MODEL), jnp.float32)
    positions = jnp.broadcast_to(jnp.arange(SEQ, dtype=jnp.int32), (BATCH, SEQ))
    norm_gain = 1.0 + 0.02 * jax.random.normal(ks[1], (DEPTH, D_MODEL), jnp.float32)
    w_in = jax.random.normal(ks[2], (DEPTH, D_MODEL, IN_COLS), jnp.float32) * D_MODEL ** -0.5
    conv_w = jax.random.normal(ks[3], (DEPTH, CONV_K, GDN_QKV_WIDTH), jnp.float32) * CONV_K ** -0.5
    a_log = jnp.log(jax.random.uniform(ks[4], (DEPTH, GDN_HEADS), jnp.float32, 1.0, 16.0))
    dt = jnp.exp(jax.random.uniform(ks[5], (DEPTH, GDN_HEADS), jnp.float32,
                                    math.log(1e-3), math.log(1e-1)))
    dt_bias = dt + jnp.log(-jnp.expm1(-dt))
    gdn_norm_gain = 1.0 + 0.02 * jax.random.normal(ks[6], (DEPTH, GDN_DV), jnp.float32)
    idx_k_gain = 1.0 + 0.02 * jax.random.normal(ks[7], (DEPTH, IDX_DIM), jnp.float32)
    w_out = jax.random.normal(ks[8], (DEPTH, D_MODEL, D_MODEL), jnp.float32) * D_MODEL ** -0.5
    final_gain = 1.0 + 0.02 * jax.random.normal(ks[9], (D_MODEL,), jnp.float32)
    return {"x": x, "positions": positions, "norm_gain": norm_gain, "w_in": w_in,
            "conv_w": conv_w, "a_log": a_log, "dt_bias": dt_bias,
            "gdn_norm_gain": gdn_norm_gain, "idx_k_gain": idx_k_gain,
            "w_out": w_out, "final_gain": final_gain}


def reference(x, positions, norm_gain, w_in, conv_w, a_log, dt_bias, gdn_norm_gain,
              idx_k_gain, w_out, final_gain):
    B, S = x.shape[0], x.shape[1]
    cos_a, sin_a = rope_tables(positions, ATTN_ROT_DIM)
    cos_i, sin_i = rope_tables(positions, IDX_ROT_DIM)
    idx_scale = IDX_HEADS ** -0.5 * IDX_DIM ** -0.5
    for layer in range(DEPTH):
        h = rms_norm(x, norm_gain[layer])
        proj = h @ w_in[layer]
        (q_a, k_a, v_a, z_a, q_i, k_i, w_i, qkv_b, z_b, beta_l, a_l, gate_l) = jnp.split(
            proj, IN_SPLIT_IDX, axis=-1)

        q_a = apply_partial_rope(q_a.reshape(B, S, ATTN_HEADS, ATTN_HEAD_DIM), cos_a, sin_a)
        k_a = apply_partial_rope(k_a.reshape(B, S, ATTN_KV_HEADS, ATTN_HEAD_DIM), cos_a, sin_a)
        v_a = v_a.reshape(B, S, ATTN_KV_HEADS, ATTN_HEAD_DIM)
        q_i = apply_partial_rope(q_i.reshape(B, S, IDX_HEADS, IDX_DIM), cos_i, sin_i)
        k_i = apply_partial_rope(rms_norm(k_i, idx_k_gain[layer])[:, :, None, :],
                                 cos_i, sin_i)[:, :, 0, :]
        w_i = w_i * idx_scale
        y_a = dsa_attention(q_a, k_a, v_a, q_i, k_i, w_i) * jax.nn.silu(z_a)

        qkv_b = jax.nn.silu(causal_depthwise_conv(qkv_b, conv_w[layer]))
        q_b, k_b, v_b = jnp.split(qkv_b, [GDN_QK_WIDTH, 2 * GDN_QK_WIDTH], axis=-1)
        q_b = l2_normalize(q_b.reshape(B, S, GDN_HEADS, GDN_DK))
        k_b = l2_normalize(k_b.reshape(B, S, GDN_HEADS, GDN_DK))
        v_b = v_b.reshape(B, S, GDN_HEADS, GDN_DV).astype(jnp.float32)
        beta = jax.nn.sigmoid(beta_l.astype(jnp.float32))
        g = -jnp.exp(a_log[layer].astype(jnp.float32)) * jax.nn.softplus(
            a_l.astype(jnp.float32) + dt_bias[layer].astype(jnp.float32))
        o_b = chunked_gated_delta_rule(q_b, k_b, v_b, beta, g).astype(x.dtype)
        y_b = (rms_norm(o_b, gdn_norm_gain[layer])
               * jax.nn.silu(z_b.reshape(B, S, GDN_HEADS, GDN_DV))).reshape(B, S, GDN_WIDTH)

        gates = jax.nn.sigmoid(gate_l).reshape(B, S, N_BRANCH, D_MODEL)
        mixed = gates[:, :, 0, :] * y_a + gates[:, :, 1, :] * y_b
        x = x + mixed @ w_out[layer]
    return rms_norm(x, final_gain)
```

```python
import functools
import math

import jax
import jax.numpy as jnp
from jax import lax
from jax.experimental import pallas as pl
from jax.experimental.pallas import tpu as pltpu

F32 = jnp.float32
BF16 = jnp.bfloat16
I32 = jnp.int32

D_MODEL = 1024
ATTN_HEADS = 8
ATTN_KV_HEADS = 2
HEAD_DIM = 128
ATTN_GROUP = ATTN_HEADS // ATTN_KV_HEADS
ATTN_ROT_HALF = 16
IDX_HEADS = 8
IDX_DIM = 64
IDX_ROT_HALF = 8
INDEX_TOPK = 256
ROPE_THETA = 500000.0
GDN_HEADS = 8
GDN_DK = 128
CONV_K = 4
CHUNK = 64
NORM_EPS = 1e-6
IN_SPLITS = (1024, 256, 256, 1024, 512, 64, 8, 3072, 1024, 8, 8, 2048)

LANES = 128
SUBLANES = 8
VMEM_LIMIT = 56 * 1024 * 1024

COL_QA, COL_ZA, COL_ZB, COL_GA, COL_GB = 0, 1024, 2048, 3072, 4096
COL_QB, COL_KB, COL_VB, COL_QI = 5120, 6144, 7168, 8192
COL_KA, COL_VA, COL_KI, COL_MISC = 9216, 9472, 9728, 9856
P_COLS = 9984
MISC_WI, MISC_BETA, MISC_A = 0, 8, 16

MXU_DTYPE = BF16
GDN_PRECISION = lax.Precision.HIGHEST

KEY_MIN = -2 ** 31
NEG_BIG = -1e30


def _params(sem):
    return pltpu.CompilerParams(dimension_semantics=sem, vmem_limit_bytes=VMEM_LIMIT)


def _proj_kernel(x_ref, gain_ref, w_ref, o_ref, h_ref):
    @pl.when(pl.program_id(1) == 0)
    def _():
        x = x_ref[...]
        ms = jnp.mean(x * x, axis=-1, keepdims=True)
        h_ref[...] = (x * lax.rsqrt(ms + NORM_EPS) * gain_ref[...]).astype(h_ref.dtype)

    o_ref[...] = jnp.dot(h_ref[...], w_ref[...], preferred_element_type=F32)


def _project(x2d, gain, w_packed, tm, tn):
    m = x2d.shape[0]
    return pl.pallas_call(
        _proj_kernel,
        grid=(m // tm, P_COLS // tn),
        in_specs=[
            pl.BlockSpec((tm, D_MODEL), lambda i, j: (i, 0)),
            pl.BlockSpec((1, D_MODEL), lambda i, j: (0, 0)),
            pl.BlockSpec((D_MODEL, tn), lambda i, j: (0, j)),
        ],
        out_specs=pl.BlockSpec((tm, tn), lambda i, j: (i, j)),
        out_shape=jax.ShapeDtypeStruct((m, P_COLS), F32),
        scratch_shapes=[pltpu.VMEM((tm, D_MODEL), MXU_DTYPE)],
        compiler_params=_params(("parallel", "arbitrary")),
        name="in_proj",
    )(x2d, gain, w_packed)


def _rope_tab_kernel(pos_ref, inva_ref, invi_ref, cosa_ref, sina_ref, cosi_ref, sini_ref):
    pos = pos_ref[...].astype(F32)
    lane = lax.broadcasted_iota(I32, (1, LANES), 1)

    def tables(inv, half, cos_ref, sin_ref):
        ang = pos * inv
        c = jnp.cos(ang)
        s = jnp.sin(ang)
        cos_ref[...] = jnp.where(lane < 2 * half, c, 1.0)
        sin_ref[...] = jnp.where(lane < half, -s, jnp.where(lane < 2 * half, s, 0.0))

    tables(inva_ref[...], ATTN_ROT_HALF, cosa_ref, sina_ref)
    tables(invi_ref[...], IDX_ROT_HALF, cosi_ref, sini_ref)


def _rope_tables(pos2d, inv_a, inv_i, tm):
    m = pos2d.shape[0]
    tab = jax.ShapeDtypeStruct((m, LANES), F32)
    row = pl.BlockSpec((tm, LANES), lambda i: (i, 0))
    one = pl.BlockSpec((1, LANES), lambda i: (0, 0))
    return pl.pallas_call(
        _rope_tab_kernel,
        grid=(m // tm,),
        in_specs=[pl.BlockSpec((tm, 1), lambda i: (i, 0)), one, one],
        out_specs=[row, row, row, row],
        out_shape=[tab, tab, tab, tab],
        compiler_params=_params(("parallel",)),
        name="rope_tables",
    )(pos2d, inv_a, inv_i)


def _rope(x, c, s, half):
    lane = lax.broadcasted_iota(I32, x.shape, 1)
    partner = jnp.where(lane < half, pltpu.roll(x, LANES - half, 1), pltpu.roll(x, half, 1))
    return x * c + partner * s


def _attn_prep_kernel(qa_ref, ka_ref, va_ref, qi_ref, ki_ref, misc_ref,
                      cosa_ref, sina_ref, cosi_ref, sini_ref, kgain_ref,
                      q_out, k_out, v_out, qi_out, ki_out, wi_out):
    ca, sa = cosa_ref[...], sina_ref[...]
    ci, si = cosi_ref[...], sini_ref[...]
    for h in range(ATTN_HEADS):
        sl = slice(h * LANES, (h + 1) * LANES)
        q_out[:, sl] = _rope(qa_ref[:, sl], ca, sa, ATTN_ROT_HALF).astype(q_out.dtype)
    for h in range(ATTN_KV_HEADS):
        sl = slice(h * LANES, (h + 1) * LANES)
        k_out[:, sl] = _rope(ka_ref[:, sl], ca, sa, ATTN_ROT_HALF).astype(k_out.dtype)
    v_out[...] = va_ref[...].astype(v_out.dtype)
    for h in range(IDX_HEADS):
        sl = slice(h * LANES, (h + 1) * LANES)
        qi_out[:, sl] = _rope(qi_ref[:, sl], ci, si, IDX_ROT_HALF).astype(qi_out.dtype)
    ki = ki_ref[...]
    ms = jnp.sum(ki * ki, axis=-1, keepdims=True) * (1.0 / IDX_DIM)
    ki = ki * lax.rsqrt(ms + NORM_EPS) * kgain_ref[...]
    ki_out[...] = _rope(ki, ci, si, IDX_ROT_HALF).astype(ki_out.dtype)
    wi_out[...] = misc_ref[...] * (IDX_HEADS ** -0.5 * IDX_DIM ** -0.5)


def _attn_prep(p, tabs, kgain, tm):
    m = p.shape[0]
    cosa, sina, cosi, sini = tabs

    def col(width, off):
        blk = off // width
        return pl.BlockSpec((tm, width), lambda i: (i, blk))

    row = pl.BlockSpec((tm, LANES), lambda i: (i, 0))
    outs = [(ATTN_HEADS * LANES, MXU_DTYPE), (ATTN_KV_HEADS * LANES, MXU_DTYPE),
            (ATTN_KV_HEADS * LANES, MXU_DTYPE), (IDX_HEADS * LANES, MXU_DTYPE),
            (LANES, MXU_DTYPE), (LANES, F32)]
    return pl.pallas_call(
        _attn_prep_kernel,
        grid=(m // tm,),
        in_specs=[col(1024, COL_QA), col(256, COL_KA), col(256, COL_VA), col(1024, COL_QI),
                  col(128, COL_KI), col(128, COL_MISC), row, row, row, row,
                  pl.BlockSpec((1, LANES), lambda i: (0, 0))],
        out_specs=[pl.BlockSpec((tm, w), lambda i: (i, 0)) for w, _ in outs],
        out_shape=[jax.ShapeDtypeStruct((m, w), d) for w, d in outs],
        compiler_params=_params(("parallel",)),
        name="attn_prep",
    )(p, p, p, p, p, p, cosa, sina, cosi, sini, kgain)


def _nt_dot(a, b):
    return lax.dot_general(a, b, (((1,), (1,)), ((), ())), preferred_element_type=F32)


def _dsa_kernel(q_ref, qi_ref, wi_ref, k_ref, v_ref, kidx_ref, u_ref, o_ref,
                keys_ref, m_ref, l_ref, acc_ref, *, tq, tk, topk, scale):
    i = pl.program_id(1)
    nkt = (i * tq + tq + tk - 1) // tk
    row_t = i * tq + lax.broadcasted_iota(I32, (tq, 1), 0)
    grp = ATTN_GROUP * tq

    qi = qi_ref[...]
    wi = wi_ref[...]

    def idx_body(kt, carry):
        off = pl.multiple_of(kt * tk, tk)
        kk = kidx_ref[pl.ds(off, tk), :]
        acc = jnp.zeros((tq, tk), F32)
        for h in range(IDX_HEADS):
            raw = _nt_dot(qi[:, h * LANES:(h + 1) * LANES], kk)
            acc = acc + jnp.maximum(raw, 0.0) * wi[:, MISC_WI + h:MISC_WI + h + 1]
        bits = pltpu.bitcast(acc, I32)
        key = jnp.where(bits < 0, (bits ^ jnp.int32(0x7FFFFFFF)) + 1, bits)
        col = off + lax.broadcasted_iota(I32, (1, tk), 1)
        keys_ref[kt] = jnp.where(col <= row_t, key, jnp.int32(KEY_MIN))
        return carry

    lax.fori_loop(0, nkt, idx_body, 0)

    def count_ge(cand):
        candb = jnp.broadcast_to(cand, (tq, LANES))

        def body(kt, acc):
            kk = keys_ref[kt]
            for c in range(tk // LANES):
                acc = acc + jnp.where(kk[:, c * LANES:(c + 1) * LANES] >= candb, 1.0, 0.0)
            return acc

        acc = lax.fori_loop(0, nkt, body, jnp.zeros((tq, LANES), F32))
        return jnp.sum(acc, axis=1, keepdims=True)

    def bit_body(b, ans):
        cand = ans + lax.shift_left(jnp.int32(1), 31 - b)
        return jnp.where(count_ge(cand) >= float(topk), cand, ans)

    thr = lax.fori_loop(0, 32, bit_body, jnp.full((tq, 1), KEY_MIN, I32))
    need = jnp.where(thr == jnp.int32(KEY_MIN), 0.0, float(topk) - count_ge(thr + 1))

    q = q_ref[...]
    qg = [jnp.concatenate([q[:, (g * ATTN_GROUP + hh) * LANES:(g * ATTN_GROUP + hh + 1) * LANES]
                           for hh in range(ATTN_GROUP)], axis=0) for g in range(ATTN_KV_HEADS)]
    m_ref[...] = jnp.full(m_ref.shape, NEG_BIG, F32)
    l_ref[...] = jnp.zeros(l_ref.shape, F32)
    acc_ref[...] = jnp.zeros(acc_ref.shape, F32)

    def attn_body(kt, tie_carry):
        off = pl.multiple_of(kt * tk, tk)
        kk = keys_ref[kt]
        gt = kk > thr
        eq = kk == thr
        prefix = jnp.dot(jnp.where(eq, 1.0, 0.0).astype(BF16), u_ref[...],
                         preferred_element_type=F32)
        rank_ok = (prefix + tie_carry) <= need
        bias = jnp.where(gt, 0.0, jnp.where(eq, jnp.where(rank_ok, 0.0, NEG_BIG), NEG_BIG))
        for g in range(ATTN_KV_HEADS):
            k_t = k_ref[pl.ds(off, tk), g * LANES:(g + 1) * LANES]
            v_t = v_ref[pl.ds(off, tk), g * LANES:(g + 1) * LANES]
            s = _nt_dot(qg[g], k_t) * scale
            s = (s.reshape(ATTN_GROUP, tq, tk) + bias[None]).reshape(grp, tk)
            m_prev = m_ref[g]
            m_new = jnp.maximum(m_prev, jnp.max(s, axis=1, keepdims=True))
            alpha = jnp.exp(m_prev - m_new)
            p = jnp.exp(s - m_new)
            l_ref[g] = alpha * l_ref[g] + jnp.sum(p, axis=1, keepdims=True)
            acc_ref[g] = alpha * acc_ref[g] + jnp.dot(p.astype(v_t.dtype), v_t,
                                                      preferred_element_type=F32)
            m_ref[g] = m_new
        return tie_carry + prefix[:, tk - 1:tk]

    lax.fori_loop(0, nkt, attn_body, jnp.zeros((tq, 1), F32))

    for g in range(ATTN_KV_HEADS):
        o = acc_ref[g] / l_ref[g]
        for hh in range(ATTN_GROUP):
            h = g * ATTN_GROUP + hh
            o_ref[:, h * LANES:(h + 1) * LANES] = o[hh * tq:(hh + 1) * tq].astype(o_ref.dtype)


def _dsa(q, k, v, qi, ki, wi, tie_u, tq, tk):
    b, s, _ = q.shape
    topk = min(INDEX_TOPK, s // 4)
    grp = ATTN_GROUP * tq
    kern = functools.partial(_dsa_kernel, tq=tq, tk=tk, topk=topk, scale=HEAD_DIM ** -0.5)

    def qblk(w):
        return pl.BlockSpec((None, tq, w), lambda bi, i: (bi, i, 0))

    def full(w):
        return pl.BlockSpec((None, s, w), lambda bi, i: (bi, 0, 0))

    return pl.pallas_call(
        kern,
        grid=(b, s // tq),
        in_specs=[qblk(ATTN_HEADS * LANES), qblk(IDX_HEADS * LANES), qblk(LANES),
                  full(ATTN_KV_HEADS * LANES), full(ATTN_KV_HEADS * LANES), full(LANES),
                  pl.BlockSpec((tk, tk), lambda bi, i: (0, 0))],
        out_specs=qblk(ATTN_HEADS * LANES),
        out_shape=jax.ShapeDtypeStruct((b, s, ATTN_HEADS * LANES), F32),
        scratch_shapes=[pltpu.VMEM((s // tk, tq, tk), I32),
                        pltpu.VMEM((ATTN_KV_HEADS, grp, 1), F32),
                        pltpu.VMEM((ATTN_KV_HEADS, grp, 1), F32),
                        pltpu.VMEM((ATTN_KV_HEADS, grp, LANES), F32)],
        compiler_params=_params(("parallel", "arbitrary")),
        name="dsa_attention",
    )(q, qi, wi, k, v, ki, tie_u)


def _shift_rows(x, halo, d):
    rolled = pltpu.roll(x, d, 0)
    head_rows = lax.broadcasted_iota(I32, (SUBLANES, x.shape[1]), 0)
    first = jnp.where(head_rows < d, pltpu.roll(halo, d, 0), rolled[:SUBLANES])
    return jnp.concatenate([first, rolled[SUBLANES:]], axis=0)


def _gdn_prep_kernel(q_ref, k_ref, v_ref, qh_ref, kh_ref, vh_ref, misc_ref,
                     wq_ref, wk_ref, wv_ref, alog_ref, dtb_ref,
                     qn_out, kn_out, vv_out, gcol_out, gt_out, *, tm, seq):
    seq_start = (pl.program_id(0) * tm) % seq == 0

    def conv_silu(x_ref, halo_ref, w_ref):
        x = x_ref[...]
        halo = jnp.where(seq_start, 0.0, halo_ref[...])
        w = w_ref[...]
        y = x * w[CONV_K - 1:CONV_K]
        for d in range(1, CONV_K):
            y = y + _shift_rows(x, halo, d) * w[CONV_K - 1 - d:CONV_K - d]
        return y * jax.nn.sigmoid(y)

    def l2norm(x, out_ref, mult):
        for h in range(GDN_HEADS):
            sl = slice(h * LANES, (h + 1) * LANES)
            xh = x[:, sl]
            ss = jnp.sum(xh * xh, axis=-1, keepdims=True)
            out_ref[:, sl] = xh * (lax.rsqrt(ss + NORM_EPS) * mult)

    l2norm(conv_silu(q_ref, qh_ref, wq_ref), qn_out, GDN_DK ** -0.5)
    l2norm(conv_silu(k_ref, kh_ref, wk_ref), kn_out, 1.0)
    vv_out[...] = conv_silu(v_ref, vh_ref, wv_ref)

    misc = misc_ref[...]
    lane = lax.broadcasted_iota(I32, misc.shape, 1)
    row = lax.broadcasted_iota(I32, misc.shape, 0)
    is_a = (lane >= MISC_A) & (lane < MISC_A + GDN_HEADS)
    z = misc + dtb_ref[...]
    softplus = jnp.maximum(z, 0.0) + jnp.log(1.0 + jnp.exp(-jnp.abs(z)))
    g = jnp.where(is_a, -jnp.exp(alog_ref[...]) * softplus, 0.0)
    sh = 1
    while sh < CHUNK:
        g = g + jnp.where((row % CHUNK) >= sh, pltpu.roll(g, sh, 0), 0.0)
        sh *= 2
    is_beta = (lane >= MISC_BETA) & (lane < MISC_BETA + GDN_HEADS)
    gcol = jnp.where(is_beta, jax.nn.sigmoid(misc), g)
    gcol_out[...] = gcol
    gt_out[...] = gcol.T


def _gdn_prep(p, conv_w, alog_pad, dtb_pad, tm, seq):
    m = p.shape[0]
    wide = GDN_HEADS * LANES
    halo_blocks = tm // SUBLANES

    def col(off):
        blk = off // wide
        return pl.BlockSpec((tm, wide), lambda i: (i, blk))

    def halo(off):
        blk = off // wide
        return pl.BlockSpec((SUBLANES, wide), lambda i: (jnp.maximum(i * halo_blocks - 1, 0), blk))

    def wblk(j):
        return pl.BlockSpec((CONV_K, wide), lambda i: (0, j))

    one = pl.BlockSpec((1, LANES), lambda i: (0, 0))
    big = jax.ShapeDtypeStruct((m, wide), F32)
    kern = functools.partial(_gdn_prep_kernel, tm=tm, seq=seq)
    return pl.pallas_call(
        kern,
        grid=(m // tm,),
        in_specs=[col(COL_QB), col(COL_KB), col(COL_VB), halo(COL_QB), halo(COL_KB), halo(COL_VB),
                  pl.BlockSpec((tm, LANES), lambda i: (i, COL_MISC // LANES)),
                  wblk(0), wblk(1), wblk(2), one, one],
        out_specs=[pl.BlockSpec((tm, wide), lambda i: (i, 0))] * 3
        + [pl.BlockSpec((tm, LANES), lambda i: (i, 0)), pl.BlockSpec((LANES, tm), lambda i: (0, i))],
        out_shape=[big, big, big, jax.ShapeDtypeStruct((m, LANES), F32),
                   jax.ShapeDtypeStruct((LANES, m), F32)],
        compiler_params=_params(("parallel",)),
        name="gdn_prep",
    )(p, p, p, p, p, p, p, conv_w, conv_w, conv_w, alog_pad, dtb_pad)


def _hp_dot(a, b):
    return jnp.dot(a, b, preferred_element_type=F32, precision=GDN_PRECISION)


def _gdn_kernel(q_ref, k_ref, v_ref, gcol_ref, gt_ref, o_ref, state_ref):
    h = pl.program_id(2)

    @pl.when(pl.program_id(1) == 0)
    def _():
        state_ref[h] = jnp.zeros(state_ref.shape[1:], F32)

    rows = q_ref.shape[0]
    gcol = gcol_ref[...]
    lane = lax.broadcasted_iota(I32, gcol.shape, 1)
    beta_all = jnp.sum(jnp.where(lane == MISC_BETA + h, gcol, 0.0), axis=1, keepdims=True)
    gc_all = jnp.sum(jnp.where(lane == MISC_A + h, gcol, 0.0), axis=1, keepdims=True)
    gcr_all = gt_ref[pl.ds(MISC_A + h, 1), :]

    ii = lax.broadcasted_iota(I32, (CHUNK, CHUNK), 0)
    jj = lax.broadcasted_iota(I32, (CHUNK, CHUNK), 1)
    incl = ii >= jj
    strict = ii > jj
    eye = jnp.where(ii == jj, 1.0, 0.0)

    state = state_ref[h]
    for sub in range(rows // CHUNK):
        sl = slice(sub * CHUNK, (sub + 1) * CHUNK)
        q, k, v = q_ref[sl, :], k_ref[sl, :], v_ref[sl, :]
        beta, gc, gcr = beta_all[sl], gc_all[sl], gcr_all[:, sl]
        decay = jnp.where(incl, jnp.exp(gc - gcr), 0.0)
        kb = k * beta
        a = lax.dot_general(kb, k, (((1,), (1,)), ((), ())), preferred_element_type=F32,
                            precision=GDN_PRECISION)
        x = -jnp.where(strict, a * decay, 0.0)
        tinv = eye + x
        span = 1
        while 2 * span < CHUNK:
            x = _hp_dot(x, x)
            tinv = tinv + _hp_dot(tinv, x)
            span *= 2
        egc = jnp.exp(gc)
        u = _hp_dot(tinv, v * beta)
        w = _hp_dot(tinv, kb * egc)
        qk = lax.dot_general(q, k, (((1,), (1,)), ((), ())), preferred_element_type=F32,
                             precision=GDN_PRECISION) * decay
        g_last = gc[CHUNK - 1:CHUNK]
        k_dec = k * jnp.exp(g_last - gc)
        v_new = u - _hp_dot(w, state)
        o_ref[sl, :] = _hp_dot(q * egc, state) + _hp_dot(qk, v_new)
        state = state * jnp.exp(g_last) + lax.dot_general(
            k_dec, v_new, (((0,), (0,)), ((), ())), preferred_element_type=F32,
            precision=GDN_PRECISION)
    state_ref[h] = state


def _gdn(qn, kn, vv, gcol, gt, batch, seq, rows):
    nsteps = seq // rows

    def tok(bi, c, h):
        return (bi * nsteps + c, h)

    blk = pl.BlockSpec((rows, LANES), tok)
    return pl.pallas_call(
        _gdn_kernel,
        grid=(batch, nsteps, GDN_HEADS),
        in_specs=[blk, blk, blk,
                  pl.BlockSpec((rows, LANES), lambda bi, c, h: (bi * nsteps + c, 0)),
                  pl.BlockSpec((LANES, rows), lambda bi, c, h: (0, bi * nsteps + c))],
        out_specs=blk,
        out_shape=jax.ShapeDtypeStruct(qn.shape, F32),
        scratch_shapes=[pltpu.VMEM((GDN_HEADS, GDN_DK, LANES), F32)],
        compiler_params=_params(("parallel", "arbitrary", "arbitrary")),
        name="gdn_chunks",
    )(qn, kn, vv, gcol, gt)


def _merge_kernel(oa_ref, ob_ref, za_ref, zb_ref, ga_ref, gb_ref, x_ref, gng_ref, w_ref, fg_ref,
                  o_ref, mix_ref, *, final_norm):
    za = za_ref[...]
    ya = oa_ref[...] * (za * jax.nn.sigmoid(za))
    mix_ref[...] = (jax.nn.sigmoid(ga_ref[...]) * ya).astype(mix_ref.dtype)
    gng = gng_ref[...]
    for h in range(GDN_HEADS):
        sl = slice(h * LANES, (h + 1) * LANES)
        ob = ob_ref[:, sl]
        zb = zb_ref[:, sl]
        ms = jnp.mean(ob * ob, axis=-1, keepdims=True)
        yb = ob * lax.rsqrt(ms + NORM_EPS) * gng * (zb * jax.nn.sigmoid(zb))
        mix_ref[:, sl] += jax.nn.sigmoid(gb_ref[:, sl]) * yb
    y = x_ref[...] + jnp.dot(mix_ref[...].astype(MXU_DTYPE), w_ref[...], preferred_element_type=F32)
    if final_norm:
        ms = jnp.mean(y * y, axis=-1, keepdims=True)
        y = y * lax.rsqrt(ms + NORM_EPS) * fg_ref[...]
    o_ref[...] = y


def _merge_out(oa, ob, p, x2d, gdn_gain, w_out, final_gain, tm, final_norm):
    m = x2d.shape[0]

    def col(off):
        blk = off // D_MODEL
        return pl.BlockSpec((tm, D_MODEL), lambda i: (i, blk))

    row = pl.BlockSpec((tm, D_MODEL), lambda i: (i, 0))
    kern = functools.partial(_merge_kernel, final_norm=final_norm)
    return pl.pallas_call(
        kern,
        grid=(m // tm,),
        in_specs=[row, row, col(COL_ZA), col(COL_ZB), col(COL_GA), col(COL_GB), row,
                  pl.BlockSpec((1, LANES), lambda i: (0, 0)),
                  pl.BlockSpec((D_MODEL, D_MODEL), lambda i: (0, 0)),
                  pl.BlockSpec((1, D_MODEL), lambda i: (0, 0))],
        out_specs=row,
        out_shape=jax.ShapeDtypeStruct((m, D_MODEL), F32),
        scratch_shapes=[pltpu.VMEM((tm, D_MODEL), F32)],
        compiler_params=_params(("parallel",)),
        name="merge_out_proj",
    )(oa, ob, p, p, p, p, x2d, gdn_gain, w_out, final_gain)


def _pack_w_in(w):
    edges = [0]
    for n in IN_SPLITS:
        edges.append(edges[-1] + n)
    (qa, ka, va, za, qi, ki, wi, qkvb, zb, beta, a, gates) = [
        w[:, edges[j]:edges[j + 1]] for j in range(len(IN_SPLITS))]
    d = w.shape[0]
    qi = jnp.pad(qi.reshape(d, IDX_HEADS, IDX_DIM), ((0, 0), (0, 0), (0, LANES - IDX_DIM)))
    qi = qi.reshape(d, IDX_HEADS * LANES)
    ki = jnp.pad(ki, ((0, 0), (0, LANES - IDX_DIM)))
    misc = jnp.pad(jnp.concatenate([wi, beta, a], axis=1), ((0, 0), (0, LANES - 3 * 8)))
    packed = jnp.concatenate(
        [qa, za, zb, gates[:, :D_MODEL], gates[:, D_MODEL:], qkvb, qi, ka, va, ki, misc], axis=1)
    return packed.astype(MXU_DTYPE)


def _lane_pad(v, off):
    return jnp.pad(v.astype(F32), (off, LANES - off - v.shape[0])).reshape(1, LANES)


def _pick(n, prefs):
    for t in prefs:
        if n % t == 0:
            return t
    raise ValueError(f"no tile for {n}")


def kernel(x, positions, norm_gain, w_in, conv_w, a_log, dt_bias, gdn_norm_gain, idx_k_gain, w_out,
           final_gain):
    batch, seq, d = x.shape
    assert d == D_MODEL and seq % 128 == 0
    depth = w_in.shape[0]
    m = batch * seq
    tm_proj = _pick(m, (1024, 512, 256, 128))
    tn_proj = 768
    tm_prep = _pick(seq, (512, 256, 128))
    tm_merge = _pick(m, (256, 128))
    tq = 128
    tk = _pick(seq, (512, 256, 128))

    lane = jnp.arange(LANES)

    def inv_freq(half):
        rot = 2 * half
        f = ROPE_THETA ** (-jnp.arange(0, rot, 2, dtype=F32) / rot)
        return jnp.where(lane < rot, f[lane % half], 0.0).reshape(1, LANES).astype(F32)

    tabs = _rope_tables(positions.reshape(m, 1).astype(I32), inv_freq(ATTN_ROT_HALF),
                        inv_freq(IDX_ROT_HALF), tm_prep)
    tie_u = (jnp.arange(tk)[:, None] <= jnp.arange(tk)[None, :]).astype(BF16)

    x2d = x.reshape(m, d)
    for layer in range(depth):
        p = _project(x2d, norm_gain[layer].reshape(1, d), _pack_w_in(w_in[layer]), tm_proj, tn_proj)

        kgain = _lane_pad(idx_k_gain[layer], 0)
        q, k, v, qi, ki, wi = _attn_prep(p, tabs, kgain, tm_prep)
        shp = lambda t: t.reshape(batch, seq, t.shape[-1])
        o_a = _dsa(shp(q), shp(k), shp(v), shp(qi), shp(ki), shp(wi), tie_u, tq, tk).reshape(m, d)

        qn, kn, vv, gcol, gt = _gdn_prep(p, conv_w[layer], _lane_pad(a_log[layer], MISC_A),
                                         _lane_pad(dt_bias[layer], MISC_A), tm_prep, seq)
        o_b = _gdn(qn, kn, vv, gcol, gt, batch, seq, 2 * CHUNK)

        x2d = _merge_out(o_a, o_b, p, x2d, gdn_norm_gain[layer].reshape(1, LANES),
                         w_out[layer].astype(MXU_DTYPE), final_gain.reshape(1, d), tm_merge,
                         final_norm=(layer == depth - 1))
    return x2d.reshape(batch, seq, d)
```

```python
import functools
import math

import jax
import jax.numpy as jnp
from jax import lax
from jax.experimental import pallas as pl
from jax.experimental.pallas import tpu as pltpu

F32 = jnp.float32
BF16 = jnp.bfloat16
I32 = jnp.int32

D_MODEL = 1024
ATTN_HEADS = 8
ATTN_KV_HEADS = 2
HEAD_DIM = 128
ATTN_GROUP = ATTN_HEADS // ATTN_KV_HEADS
ATTN_ROT_HALF = 16
IDX_HEADS = 8
IDX_DIM = 64
IDX_ROT_HALF = 8
INDEX_TOPK = 256
ROPE_THETA = 500000.0
GDN_HEADS = 8
GDN_DK = 128
CONV_K = 4
CHUNK = 64
NORM_EPS = 1e-6
IN_SPLITS = (1024, 256, 256, 1024, 512, 64, 8, 3072, 1024, 8, 8, 2048)

LANES = 128
SUBLANES = 8
VMEM_LIMIT = 56 * 1024 * 1024

COL_QA, COL_ZA, COL_ZB, COL_GA, COL_GB = 0, 1024, 2048, 3072, 4096
COL_QB, COL_KB, COL_VB, COL_QI = 5120, 6144, 7168, 8192
COL_KA, COL_VA, COL_KI, COL_MISC = 9216, 9472, 9728, 9856
P_COLS = 9984
MISC_WI, MISC_BETA, MISC_A = 0, 8, 16

MXU_DTYPE = BF16
GDN_PRECISION = lax.Precision.HIGHEST

KEY_MIN = -2 ** 31
NEG_BIG = -1e30


def _params(sem):
    return pltpu.CompilerParams(dimension_semantics=sem, vmem_limit_bytes=VMEM_LIMIT)


def _proj_kernel(x_ref, gain_ref, w_ref, o_ref, h_ref):
    @pl.when(pl.program_id(1) == 0)
    def _():
        x = x_ref[...]
        ms = jnp.mean(x * x, axis=-1, keepdims=True)
        h_ref[...] = (x * lax.rsqrt(ms + NORM_EPS) * gain_ref[...]).astype(h_ref.dtype)

    o_ref[...] = jnp.dot(h_ref[...], w_ref[...], preferred_element_type=F32)


def _project(x2d, gain, w_packed, tm, tn):
    m = x2d.shape[0]
    return pl.pallas_call(
        _proj_kernel,
        grid=(m // tm, P_COLS // tn),
        in_specs=[
            pl.BlockSpec((tm, D_MODEL), lambda i, j: (i, 0)),
            pl.BlockSpec((1, D_MODEL), lambda i, j: (0, 0)),
            pl.BlockSpec((D_MODEL, tn), lambda i, j: (0, j)),
        ],
        out_specs=pl.BlockSpec((tm, tn), lambda i, j: (i, j)),
        out_shape=jax.ShapeDtypeStruct((m, P_COLS), F32),
        scratch_shapes=[pltpu.VMEM((tm, D_MODEL), MXU_DTYPE)],
        compiler_params=_params(("parallel", "arbitrary")),
        name="in_proj",
    )(x2d, gain, w_packed)


def _rope_tab_kernel(pos_ref, inva_ref, invi_ref, cosa_ref, sina_ref, cosi_ref, sini_ref):
    pos = pos_ref[...].astype(F32)
    lane = lax.broadcasted_iota(I32, (1, LANES), 1)

    def tables(inv, half, cos_ref, sin_ref):
        ang = pos * inv
        c = jnp.cos(ang)
        s = jnp.sin(ang)
        cos_ref[...] = jnp.where(lane < 2 * half, c, 1.0)
        sin_ref[...] = jnp.where(lane < half, -s, jnp.where(lane < 2 * half, s, 0.0))

    tables(inva_ref[...], ATTN_ROT_HALF, cosa_ref, sina_ref)
    tables(invi_ref[...], IDX_ROT_HALF, cosi_ref, sini_ref)


def _rope_tables(pos2d, inv_a, inv_i, tm):
    m = pos2d.shape[0]
    tab = jax.ShapeDtypeStruct((m, LANES), F32)
    row = pl.BlockSpec((tm, LANES), lambda i: (i, 0))
    one = pl.BlockSpec((1, LANES), lambda i: (0, 0))
    return pl.pallas_call(
        _rope_tab_kernel,
        grid=(m // tm,),
        in_specs=[pl.BlockSpec((tm, 1), lambda i: (i, 0)), one, one],
        out_specs=[row, row, row, row],
        out_shape=[tab, tab, tab, tab],
        compiler_params=_params(("parallel",)),
        name="rope_tables",
    )(pos2d, inv_a, inv_i)


def _rope(x, c, s, half):
    lane = lax.broadcasted_iota(I32, x.shape, 1)
    partner = jnp.where(lane < half, pltpu.roll(x, LANES - half, 1), pltpu.roll(x, half, 1))
    return x * c + partner * s


def _attn_prep_kernel(qa_ref, ka_ref, va_ref, qi_ref, ki_ref, misc_ref,
                      cosa_ref, sina_ref, cosi_ref, sini_ref, kgain_ref,
                      q_out, k_out, v_out, qi_out, ki_out, wi_out):
    ca, sa = cosa_ref[...], sina_ref[...]
    ci, si = cosi_ref[...], sini_ref[...]
    for h in range(ATTN_HEADS):
        sl = slice(h * LANES, (h + 1) * LANES)
        q_out[:, sl] = _rope(qa_ref[:, sl], ca, sa, ATTN_ROT_HALF).astype(q_out.dtype)
    for h in range(ATTN_KV_HEADS):
        sl = slice(h * LANES, (h + 1) * LANES)
        k_out[:, sl] = _rope(ka_ref[:, sl], ca, sa, ATTN_ROT_HALF).astype(k_out.dtype)
    v_out[...] = va_ref[...].astype(v_out.dtype)
    for h in range(IDX_HEADS):
        sl = slice(h * LANES, (h + 1) * LANES)
        qi_out[:, sl] = _rope(qi_ref[:, sl], ci, si, IDX_ROT_HALF).astype(qi_out.dtype)
    ki = ki_ref[...]
    ms = jnp.sum(ki * ki, axis=-1, keepdims=True) * (1.0 / IDX_DIM)
    ki = ki * lax.rsqrt(ms + NORM_EPS) * kgain_ref[...]
    ki_out[...] = _rope(ki, ci, si, IDX_ROT_HALF).astype(ki_out.dtype)
    wi_out[...] = misc_ref[...] * (IDX_HEADS ** -0.5 * IDX_DIM ** -0.5)


def _attn_prep(p, tabs, kgain, tm):
    m = p.shape[0]
    cosa, sina, cosi, sini = tabs

    def col(width, off):
        blk = off // width
        return pl.BlockSpec((tm, width), lambda i: (i, blk))

    row = pl.BlockSpec((tm, LANES), lambda i: (i, 0))
    outs = [(ATTN_HEADS * LANES, MXU_DTYPE), (ATTN_KV_HEADS * LANES, MXU_DTYPE),
            (ATTN_KV_HEADS * LANES, MXU_DTYPE), (IDX_HEADS * LANES, MXU_DTYPE),
            (LANES, MXU_DTYPE), (LANES, F32)]
    return pl.pallas_call(
        _attn_prep_kernel,
        grid=(m // tm,),
        in_specs=[col(1024, COL_QA), col(256, COL_KA), col(256, COL_VA), col(1024, COL_QI),
                  col(128, COL_KI), col(128, COL_MISC), row, row, row, row,
                  pl.BlockSpec((1, LANES), lambda i: (0, 0))],
        out_specs=[pl.BlockSpec((tm, w), lambda i: (i, 0)) for w, _ in outs],
        out_shape=[jax.ShapeDtypeStruct((m, w), d) for w, d in outs],
        compiler_params=_params(("parallel",)),
        name="attn_prep",
    )(p, p, p, p, p, p, cosa, sina, cosi, sini, kgain)


def _nt_dot(a, b):
    return lax.dot_general(a, b, (((1,), (1,)), ((), ())), preferred_element_type=F32)


def _dsa_kernel(q_ref, qi_ref, wi_ref, k_ref, v_ref, kidx_ref, u_ref, o_ref,
                keys_ref, m_ref, l_ref, acc_ref, *, tq, tk, topk, scale):
    i = pl.program_id(1)
    nkt = (i * tq + tq + tk - 1) // tk
    row_t = i * tq + lax.broadcasted_iota(I32, (tq, 1), 0)
    grp = ATTN_GROUP * tq

    qi = qi_ref[...]
    wi = wi_ref[...]

    def idx_body(kt, carry):
        off = pl.multiple_of(kt * tk, tk)
        kk = kidx_ref[pl.ds(off, tk), :]
        acc = jnp.zeros((tq, tk), F32)
        for h in range(IDX_HEADS):
            raw = _nt_dot(qi[:, h * LANES:(h + 1) * LANES], kk)
            acc = acc + jnp.maximum(raw, 0.0) * wi[:, MISC_WI + h:MISC_WI + h + 1]
        bits = pltpu.bitcast(acc, I32)
        key = jnp.where(bits < 0, (bits ^ jnp.int32(0x7FFFFFFF)) + 1, bits)
        col = off + lax.broadcasted_iota(I32, (1, tk), 1)
        keys_ref[kt] = jnp.where(col <= row_t, key, jnp.int32(KEY_MIN))
        return carry

    lax.fori_loop(0, nkt, idx_body, 0)

    def count_ge(cand):
        candb = jnp.broadcast_to(cand, (tq, LANES))

        def body(kt, acc):
            kk = keys_ref[kt]
            for c in range(tk // LANES):
                acc = acc + jnp.where(kk[:, c * LANES:(c + 1) * LANES] >= candb, 1.0, 0.0)
            return acc

        acc = lax.fori_loop(0, nkt, body, jnp.zeros((tq, LANES), F32))
        return jnp.sum(acc, axis=1, keepdims=True)

    def bit_body(b, ans):
        cand = ans + lax.shift_left(jnp.int32(1), 31 - b)
        return jnp.where(count_ge(cand) >= float(topk), cand, ans)

    thr = lax.fori_loop(0, 32, bit_body, jnp.full((tq, 1), KEY_MIN, I32))
    need = jnp.where(thr == jnp.int32(KEY_MIN), 0.0, float(topk) - count_ge(thr + 1))

    q = q_ref[...]
    qg = [jnp.concatenate([q[:, (g * ATTN_GROUP + hh) * LANES:(g * ATTN_GROUP + hh + 1) * LANES]
                           for hh in range(ATTN_GROUP)], axis=0) for g in range(ATTN_KV_HEADS)]
    m_ref[...] = jnp.full(m_ref.shape, NEG_BIG, F32)
    l_ref[...] = jnp.zeros(l_ref.shape, F32)
    acc_ref[...] = jnp.zeros(acc_ref.shape, F32)

    def attn_body(kt, tie_carry):
        off = pl.multiple_of(kt * tk, tk)
        kk = keys_ref[kt]
        gt = kk > thr
        eq = kk == thr
        prefix = jnp.dot(jnp.where(eq, 1.0, 0.0).astype(BF16), u_ref[...],
                         preferred_element_type=F32)
        rank_ok = (prefix + tie_carry) <= need
        bias = jnp.where(gt, 0.0, jnp.where(eq, jnp.where(rank_ok, 0.0, NEG_BIG), NEG_BIG))
        for g in range(ATTN_KV_HEADS):
            k_t = k_ref[pl.ds(off, tk), g * LANES:(g + 1) * LANES]
            v_t = v_ref[pl.ds(off, tk), g * LANES:(g + 1) * LANES]
            s = _nt_dot(qg[g], k_t) * scale
            s = (s.reshape(ATTN_GROUP, tq, tk) + bias[None]).reshape(grp, tk)
            m_prev = m_ref[g]
            m_new = jnp.maximum(m_prev, jnp.max(s, axis=1, keepdims=True))
            alpha = jnp.exp(m_prev - m_new)
            p = jnp.exp(s - m_new)
            l_ref[g] = alpha * l_ref[g] + jnp.sum(p, axis=1, keepdims=True)
            acc_ref[g] = alpha * acc_ref[g] + jnp.dot(p.astype(v_t.dtype), v_t,
                                                      preferred_element_type=F32)
            m_ref[g] = m_new
        return tie_carry + prefix[:, tk - 1:tk]

    lax.fori_loop(0, nkt, attn_body, jnp.zeros((tq, 1), F32))

    for g in range(ATTN_KV_HEADS):
        o = acc_ref[g] / l_ref[g]
        for hh in range(ATTN_GROUP):
            h = g * ATTN_GROUP + hh
            o_ref[:, h * LANES:(h + 1) * LANES] = o[hh * tq:(hh + 1) * tq].astype(o_ref.dtype)


def _dsa(q, k, v, qi, ki, wi, tie_u, tq, tk):
    b, s, _ = q.shape
    topk = min(INDEX_TOPK, s // 4)
    grp = ATTN_GROUP * tq
    kern = functools.partial(_dsa_kernel, tq=tq, tk=tk, topk=topk, scale=HEAD_DIM ** -0.5)

    def qblk(w):
        return pl.BlockSpec((None, tq, w), lambda bi, i: (bi, i, 0))

    def full(w):
        return pl.BlockSpec((None, s, w), lambda bi, i: (bi, 0, 0))

    return pl.pallas_call(
        kern,
        grid=(b, s // tq),
        in_specs=[qblk(ATTN_HEADS * LANES), qblk(IDX_HEADS * LANES), qblk(LANES),
                  full(ATTN_KV_HEADS * LANES), full(ATTN_KV_HEADS * LANES), full(LANES),
                  pl.BlockSpec((tk, tk), lambda bi, i: (0, 0))],
        out_specs=qblk(ATTN_HEADS * LANES),
        out_shape=jax.ShapeDtypeStruct((b, s, ATTN_HEADS * LANES), F32),
        scratch_shapes=[pltpu.VMEM((s // tk, tq, tk), I32),
                        pltpu.VMEM((ATTN_KV_HEADS, grp, 1), F32),
                        pltpu.VMEM((ATTN_KV_HEADS, grp, 1), F32),
                        pltpu.VMEM((ATTN_KV_HEADS, grp, LANES), F32)],
        compiler_params=_params(("parallel", "arbitrary")),
        name="dsa_attention",
    )(q, qi, wi, k, v, ki, tie_u)


def _shift_rows(x, halo, d):
    rolled = pltpu.roll(x, d, 0)
    head_rows = lax.broadcasted_iota(I32, (SUBLANES, x.shape[1]), 0)
    first = jnp.where(head_rows < d, pltpu.roll(halo, d, 0), rolled[:SUBLANES])
    return jnp.concatenate([first, rolled[SUBLANES:]], axis=0)


def _gdn_prep_kernel(q_ref, k_ref, v_ref, qh_ref, kh_ref, vh_ref, misc_ref,
                     wq_ref, wk_ref, wv_ref, alog_ref, dtb_ref,
                     qn_out, kn_out, vv_out, gcol_out, *, tm, seq):
    seq_start = (pl.program_id(0) * tm) % seq == 0

    def conv_silu(x_ref, halo_ref, w_ref):
        x = x_ref[...]
        halo = jnp.where(seq_start, 0.0, halo_ref[...])
        w = w_ref[...]
        y = x * w[CONV_K - 1:CONV_K]
        for d in range(1, CONV_K):
            y = y + _shift_rows(x, halo, d) * w[CONV_K - 1 - d:CONV_K - d]
        return y * jax.nn.sigmoid(y)

    def l2norm(x, out_ref, mult):
        for h in range(GDN_HEADS):
            sl = slice(h * LANES, (h + 1) * LANES)
            xh = x[:, sl]
            ss = jnp.sum(xh * xh, axis=-1, keepdims=True)
            out_ref[:, sl] = xh * (lax.rsqrt(ss + NORM_EPS) * mult)

    l2norm(conv_silu(q_ref, qh_ref, wq_ref), qn_out, GDN_DK ** -0.5)
    l2norm(conv_silu(k_ref, kh_ref, wk_ref), kn_out, 1.0)
    vv_out[...] = conv_silu(v_ref, vh_ref, wv_ref)

    misc = misc_ref[...]
    lane = lax.broadcasted_iota(I32, misc.shape, 1)
    row = lax.broadcasted_iota(I32, misc.shape, 0)
    is_a = (lane >= MISC_A) & (lane < MISC_A + GDN_HEADS)
    z = misc + dtb_ref[...]
    softplus = jnp.maximum(z, 0.0) + jnp.log(1.0 + jnp.exp(-jnp.abs(z)))
    g = jnp.where(is_a, -jnp.exp(alog_ref[...]) * softplus, 0.0)
    sh = 1
    while sh < CHUNK:
        g = g + jnp.where((row % CHUNK) >= sh, pltpu.roll(g, sh, 0), 0.0)
        sh *= 2
    is_beta = (lane >= MISC_BETA) & (lane < MISC_BETA + GDN_HEADS)
    gcol_out[...] = jnp.where(is_beta, jax.nn.sigmoid(misc), g)


def _gdn_prep(p, conv_w, alog_pad, dtb_pad, tm, seq):
    m = p.shape[0]
    wide = GDN_HEADS * LANES
    halo_blocks = tm // SUBLANES

    def col(off):
        blk = off // wide
        return pl.BlockSpec((tm, wide), lambda i: (i, blk))

    def halo(off):
        blk = off // wide
        return pl.BlockSpec((SUBLANES, wide), lambda i: (jnp.maximum(i * halo_blocks - 1, 0), blk))

    def wblk(j):
        return pl.BlockSpec((CONV_K, wide), lambda i: (0, j))

    one = pl.BlockSpec((1, LANES), lambda i: (0, 0))
    big = jax.ShapeDtypeStruct((m, wide), F32)
    kern = functools.partial(_gdn_prep_kernel, tm=tm, seq=seq)
    return pl.pallas_call(
        kern,
        grid=(m // tm,),
        in_specs=[col(COL_QB), col(COL_KB), col(COL_VB), halo(COL_QB), halo(COL_KB), halo(COL_VB),
                  pl.BlockSpec((tm, LANES), lambda i: (i, COL_MISC // LANES)),
                  wblk(0), wblk(1), wblk(2), one, one],
        out_specs=[pl.BlockSpec((tm, wide), lambda i: (i, 0))] * 3
        + [pl.BlockSpec((tm, LANES), lambda i: (i, 0))],
        out_shape=[big, big, big, jax.ShapeDtypeStruct((m, LANES), F32)],
        compiler_params=_params(("parallel",)),
        name="gdn_prep",
    )(p, p, p, p, p, p, p, conv_w, conv_w, conv_w, alog_pad, dtb_pad)


def _split(a):
    hi = a.astype(BF16)
    return hi, (a - hi.astype(F32)).astype(BF16)


def _dot3(a_split, b_split):
    (ah, al), (bh, bl) = a_split, b_split
    m = ah.shape[0]
    both = jnp.dot(jnp.concatenate([ah, al], axis=0), bh, preferred_element_type=F32)
    return both[:m] + both[m:] + jnp.dot(ah, bl, preferred_element_type=F32)


def _mxu_dot(a, b, dims=(((1,), (0,)), ((), ()))):
    return lax.dot_general(a.astype(MXU_DTYPE), b.astype(MXU_DTYPE), dims,
                           preferred_element_type=F32)


def _gdn_kernel(q_ref, k_ref, v_ref, gcol_ref, o_ref, state_ref):
    h = pl.program_id(0)

    @pl.when(pl.program_id(1) == 0)
    def _():
        state_ref[...] = jnp.zeros(state_ref.shape, F32)

    nb, rows = q_ref.shape[0], q_ref.shape[1]
    ii = lax.broadcasted_iota(I32, (CHUNK, CHUNK), 0)
    jj = lax.broadcasted_iota(I32, (CHUNK, CHUNK), 1)
    incl = ii >= jj
    strict = ii > jj
    eye = jnp.where(ii == jj, 1.0, 0.0)
    nt = (((1,), (1,)), ((), ()))
    tn = (((0,), (0,)), ((), ()))

    nsub = rows // CHUNK
    chains = [(b, sub) for b in range(nb) for sub in range(nsub)]
    col = {}
    for b in range(nb):
        gcol = gcol_ref[b]
        lane = lax.broadcasted_iota(I32, gcol.shape, 1)
        beta_all = jnp.sum(jnp.where(lane == MISC_BETA + h, gcol, 0.0), axis=1, keepdims=True)
        gc_all = jnp.sum(jnp.where(lane == MISC_A + h, gcol, 0.0), axis=1, keepdims=True)
        gc_sq = jnp.broadcast_to(gc_all, (rows, rows))
        col[b] = (beta_all, gc_all, gc_sq, gc_sq.T)

    xs, tinvs, rest = [], [], []
    for b, sub in chains:
        sl = slice(sub * CHUNK, (sub + 1) * CHUNK)
        beta_all, gc_all, gc_sq, gcr_sq = col[b]
        q, k, v = q_ref[b, sl, :], k_ref[b, sl, :], v_ref[b, sl, :]
        beta, gc = beta_all[sl], gc_all[sl]
        decay = jnp.where(incl, jnp.exp(gc_sq[sl, sl] - gcr_sq[sl, sl]), 0.0)
        kb = k * beta
        prods = _mxu_dot(jnp.concatenate([kb, q], axis=0), k, nt)
        x = -jnp.where(strict, prods[:CHUNK] * decay, 0.0)
        egc = jnp.exp(gc)
        g_last = gc[CHUNK - 1:CHUNK]
        xs.append(x)
        tinvs.append(eye + x)
        rest.append((jnp.concatenate([v * beta, kb * egc], axis=1), q * egc, prods[CHUNK:] * decay,
                     k * jnp.exp(g_last - gc), jnp.exp(g_last)))
    span = 1
    xsp = [_split(x) for x in xs]
    while 2 * span < CHUNK:
        xsp = [_split(_dot3(s, s)) for s in xsp]
        tinvs = [t + _dot3(_split(t), s) for t, s in zip(tinvs, xsp)]
        span *= 2
    uws = [_dot3(_split(t), _split(r[0])) for t, r in zip(tinvs, rest)]

    states = [state_ref[b] for b in range(nb)]
    for sub in range(nsub):
        sl = slice(sub * CHUNK, (sub + 1) * CHUNK)
        idx = [chains.index((b, sub)) for b in range(nb)]
        wss = [_mxu_dot(jnp.concatenate([uws[c][:, LANES:], rest[c][1]], axis=0), states[b])
               for b, c in enumerate(idx)]
        v_news = [uws[c][:, :LANES] - ws[:CHUNK] for c, ws in zip(idx, wss)]
        for b, c in enumerate(idx):
            o_ref[b, sl, :] = wss[b][CHUNK:] + _mxu_dot(rest[c][2], v_news[b])
        states = [states[b] * rest[c][4] + _mxu_dot(rest[c][3], v_news[b], tn)
                  for b, c in enumerate(idx)]
    for b in range(nb):
        state_ref[b] = states[b]


def _gdn(qn, kn, vv, gcol, rows):
    batch, seq, _ = qn.shape
    blk = pl.BlockSpec((batch, rows, LANES), lambda h, c: (0, c, h))
    return pl.pallas_call(
        _gdn_kernel,
        grid=(GDN_HEADS, seq // rows),
        in_specs=[blk, blk, blk, pl.BlockSpec((batch, rows, LANES), lambda h, c: (0, c, 0))],
        out_specs=blk,
        out_shape=jax.ShapeDtypeStruct(qn.shape, F32),
        scratch_shapes=[pltpu.VMEM((batch, GDN_DK, LANES), F32)],
        compiler_params=_params(("parallel", "arbitrary")),
        name="gdn_chunks",
    )(qn, kn, vv, gcol)


def _merge_kernel(oa_ref, ob_ref, za_ref, zb_ref, ga_ref, gb_ref, x_ref, gng_ref, w_ref, fg_ref,
                  o_ref, mix_ref, *, final_norm):
    za = za_ref[...]
    ya = oa_ref[...] * (za * jax.nn.sigmoid(za))
    mix_ref[...] = (jax.nn.sigmoid(ga_ref[...]) * ya).astype(mix_ref.dtype)
    gng = gng_ref[...]
    for h in range(GDN_HEADS):
        sl = slice(h * LANES, (h + 1) * LANES)
        ob = ob_ref[:, sl]
        zb = zb_ref[:, sl]
        ms = jnp.mean(ob * ob, axis=-1, keepdims=True)
        yb = ob * lax.rsqrt(ms + NORM_EPS) * gng * (zb * jax.nn.sigmoid(zb))
        mix_ref[:, sl] += jax.nn.sigmoid(gb_ref[:, sl]) * yb
    y = x_ref[...] + jnp.dot(mix_ref[...].astype(MXU_DTYPE), w_ref[...], preferred_element_type=F32)
    if final_norm:
        ms = jnp.mean(y * y, axis=-1, keepdims=True)
        y = y * lax.rsqrt(ms + NORM_EPS) * fg_ref[...]
    o_ref[...] = y


def _merge_out(oa, ob, p, x2d, gdn_gain, w_out, final_gain, tm, final_norm):
    m = x2d.shape[0]

    def col(off):
        blk = off // D_MODEL
        return pl.BlockSpec((tm, D_MODEL), lambda i: (i, blk))

    row = pl.BlockSpec((tm, D_MODEL), lambda i: (i, 0))
    kern = functools.partial(_merge_kernel, final_norm=final_norm)
    return pl.pallas_call(
        kern,
        grid=(m // tm,),
        in_specs=[row, row, col(COL_ZA), col(COL_ZB), col(COL_GA), col(COL_GB), row,
                  pl.BlockSpec((1, LANES), lambda i: (0, 0)),
                  pl.BlockSpec((D_MODEL, D_MODEL), lambda i: (0, 0)),
                  pl.BlockSpec((1, D_MODEL), lambda i: (0, 0))],
        out_specs=row,
        out_shape=jax.ShapeDtypeStruct((m, D_MODEL), F32),
        scratch_shapes=[pltpu.VMEM((tm, D_MODEL), F32)],
        compiler_params=_params(("parallel",)),
        name="merge_out_proj",
    )(oa, ob, p, p, p, p, x2d, gdn_gain, w_out, final_gain)


def _pack_w_in(w):
    edges = [0]
    for n in IN_SPLITS:
        edges.append(edges[-1] + n)
    (qa, ka, va, za, qi, ki, wi, qkvb, zb, beta, a, gates) = [
        w[:, edges[j]:edges[j + 1]] for j in range(len(IN_SPLITS))]
    d = w.shape[0]
    qi = jnp.pad(qi.reshape(d, IDX_HEADS, IDX_DIM), ((0, 0), (0, 0), (0, LANES - IDX_DIM)))
    qi = qi.reshape(d, IDX_HEADS * LANES)
    ki = jnp.pad(ki, ((0, 0), (0, LANES - IDX_DIM)))
    misc = jnp.pad(jnp.concatenate([wi, beta, a], axis=1), ((0, 0), (0, LANES - 3 * 8)))
    packed = jnp.concatenate(
        [qa, za, zb, gates[:, :D_MODEL], gates[:, D_MODEL:], qkvb, qi, ka, va, ki, misc], axis=1)
    return packed.astype(MXU_DTYPE)


def _lane_pad(v, off):
    return jnp.pad(v.astype(F32), (off, LANES - off - v.shape[0])).reshape(1, LANES)


def _pick(n, prefs):
    for t in prefs:
        if n % t == 0:
            return t
    raise ValueError(f"no tile for {n}")


def kernel(x, positions, norm_gain, w_in, conv_w, a_log, dt_bias, gdn_norm_gain, idx_k_gain, w_out,
           final_gain):
    batch, seq, d = x.shape
    assert d == D_MODEL and seq % 128 == 0
    depth = w_in.shape[0]
    m = batch * seq
    tm_proj = _pick(m, (1024, 512, 256, 128))
    tn_proj = 768
    tm_prep = _pick(seq, (512, 256, 128))
    tm_merge = _pick(m, (256, 128))
    tq = 128
    tk = _pick(seq, (512, 256, 128))

    lane = jnp.arange(LANES)

    def inv_freq(half):
        rot = 2 * half
        f = ROPE_THETA ** (-jnp.arange(0, rot, 2, dtype=F32) / rot)
        return jnp.where(lane < rot, f[lane % half], 0.0).reshape(1, LANES).astype(F32)

    tabs = _rope_tables(positions.reshape(m, 1).astype(I32), inv_freq(ATTN_ROT_HALF),
                        inv_freq(IDX_ROT_HALF), tm_prep)
    tie_u = (jnp.arange(tk)[:, None] <= jnp.arange(tk)[None, :]).astype(BF16)

    x2d = x.reshape(m, d)
    for layer in range(depth):
        p = _project(x2d, norm_gain[layer].reshape(1, d), _pack_w_in(w_in[layer]), tm_proj, tn_proj)

        kgain = _lane_pad(idx_k_gain[layer], 0)
        q, k, v, qi, ki, wi = _attn_prep(p, tabs, kgain, tm_prep)
        shp = lambda t: t.reshape(batch, seq, t.shape[-1])
        o_a = _dsa(shp(q), shp(k), shp(v), shp(qi), shp(ki), shp(wi), tie_u, tq, tk).reshape(m, d)

        qn, kn, vv, gcol = _gdn_prep(p, conv_w[layer], _lane_pad(a_log[layer], MISC_A),
                                     _lane_pad(dt_bias[layer], MISC_A), tm_prep, seq)
        o_b = _gdn(shp(qn), shp(kn), shp(vv), shp(gcol), 4 * CHUNK).reshape(m, d)

        x2d = _merge_out(o_a, o_b, p, x2d, gdn_norm_gain[layer].reshape(1, LANES),
                         w_out[layer].astype(MXU_DTYPE), final_gain.reshape(1, d), tm_merge,
                         final_norm=(layer == depth - 1))
    return x2d.reshape(batch, seq, d)
```

```python
import functools
import math

import jax
import jax.numpy as jnp
from jax import lax
from jax.experimental import pallas as pl
from jax.experimental.pallas import tpu as pltpu

F32 = jnp.float32
BF16 = jnp.bfloat16
I32 = jnp.int32

D_MODEL = 1024
ATTN_HEADS = 8
ATTN_KV_HEADS = 2
HEAD_DIM = 128
ATTN_GROUP = ATTN_HEADS // ATTN_KV_HEADS
ATTN_ROT_HALF = 16
IDX_HEADS = 8
IDX_DIM = 64
IDX_ROT_HALF = 8
INDEX_TOPK = 256
ROPE_THETA = 500000.0
GDN_HEADS = 8
GDN_DK = 128
CONV_K = 4
CHUNK = 64
NORM_EPS = 1e-6
IN_SPLITS = (1024, 256, 256, 1024, 512, 64, 8, 3072, 1024, 8, 8, 2048)

LANES = 128
SUBLANES = 8
VMEM_LIMIT = 56 * 1024 * 1024

COL_QA, COL_ZA, COL_ZB, COL_GA, COL_GB = 0, 1024, 2048, 3072, 4096
COL_QB, COL_KB, COL_VB, COL_QI = 5120, 6144, 7168, 8192
COL_KA, COL_VA, COL_KI, COL_MISC = 9216, 9472, 9728, 9856
P_COLS = 9984
MISC_WI, MISC_BETA, MISC_A = 0, 8, 16

MXU_DTYPE = BF16
GDN_PRECISION = lax.Precision.HIGHEST

KEY_MIN = -2 ** 31
NEG_BIG = -1e30
QK_SCALE_LOG2 = HEAD_DIM ** -0.5 * math.log2(math.e)


def _params(sem):
    return pltpu.CompilerParams(dimension_semantics=sem, vmem_limit_bytes=VMEM_LIMIT)


def _proj_kernel(x_ref, gain_ref, w_ref, o_ref, h_ref):
    @pl.when(pl.program_id(1) == 0)
    def _():
        x = x_ref[...]
        ms = jnp.mean(x * x, axis=-1, keepdims=True)
        h_ref[...] = (x * lax.rsqrt(ms + NORM_EPS) * gain_ref[...]).astype(h_ref.dtype)

    o_ref[...] = jnp.dot(h_ref[...], w_ref[...], preferred_element_type=F32)


def _project(x2d, gain, w_packed, tm, tn):
    m = x2d.shape[0]
    return pl.pallas_call(
        _proj_kernel,
        grid=(m // tm, P_COLS // tn),
        in_specs=[
            pl.BlockSpec((tm, D_MODEL), lambda i, j: (i, 0)),
            pl.BlockSpec((1, D_MODEL), lambda i, j: (0, 0)),
            pl.BlockSpec((D_MODEL, tn), lambda i, j: (0, j)),
        ],
        out_specs=pl.BlockSpec((tm, tn), lambda i, j: (i, j)),
        out_shape=jax.ShapeDtypeStruct((m, P_COLS), F32),
        scratch_shapes=[pltpu.VMEM((tm, D_MODEL), MXU_DTYPE)],
        compiler_params=_params(("parallel", "arbitrary")),
        name="in_proj",
    )(x2d, gain, w_packed)


def _rope_tab_kernel(pos_ref, inva_ref, invi_ref, cosa_ref, sina_ref, cosi_ref, sini_ref):
    pos = pos_ref[...].astype(F32)
    lane = lax.broadcasted_iota(I32, (1, LANES), 1)

    def tables(inv, half, cos_ref, sin_ref):
        ang = pos * inv
        c = jnp.cos(ang)
        s = jnp.sin(ang)
        cos_ref[...] = jnp.where(lane < 2 * half, c, 1.0)
        sin_ref[...] = jnp.where(lane < half, -s, jnp.where(lane < 2 * half, s, 0.0))

    tables(inva_ref[...], ATTN_ROT_HALF, cosa_ref, sina_ref)
    tables(invi_ref[...], IDX_ROT_HALF, cosi_ref, sini_ref)


def _rope_tables(pos2d, inv_a, inv_i, tm):
    m = pos2d.shape[0]
    tab = jax.ShapeDtypeStruct((m, LANES), F32)
    row = pl.BlockSpec((tm, LANES), lambda i: (i, 0))
    one = pl.BlockSpec((1, LANES), lambda i: (0, 0))
    return pl.pallas_call(
        _rope_tab_kernel,
        grid=(m // tm,),
        in_specs=[pl.BlockSpec((tm, 1), lambda i: (i, 0)), one, one],
        out_specs=[row, row, row, row],
        out_shape=[tab, tab, tab, tab],
        compiler_params=_params(("parallel",)),
        name="rope_tables",
    )(pos2d, inv_a, inv_i)


def _rope(x, c, s, half):
    lane = lax.broadcasted_iota(I32, x.shape, 1)
    partner = jnp.where(lane < half, pltpu.roll(x, LANES - half, 1), pltpu.roll(x, half, 1))
    return x * c + partner * s


def _attn_prep_kernel(qa_ref, ka_ref, va_ref, qi_ref, ki_ref, misc_ref,
                      cosa_ref, sina_ref, cosi_ref, sini_ref, kgain_ref,
                      q_out, k_out, v_out, qi_out, ki_out, wi_out):
    ca, sa = cosa_ref[...], sina_ref[...]
    ci, si = cosi_ref[...], sini_ref[...]
    for h in range(ATTN_HEADS):
        sl = slice(h * LANES, (h + 1) * LANES)
        q_out[:, sl] = (_rope(qa_ref[:, sl], ca, sa, ATTN_ROT_HALF) * QK_SCALE_LOG2).astype(q_out.dtype)
    ones = jnp.ones((va_ref.shape[0], LANES), v_out.dtype)
    for h in range(ATTN_KV_HEADS):
        sl = slice(h * LANES, (h + 1) * LANES)
        k_out[:, sl] = _rope(ka_ref[:, sl], ca, sa, ATTN_ROT_HALF).astype(k_out.dtype)
        v_out[:, 2 * h * LANES:(2 * h + 1) * LANES] = va_ref[:, sl].astype(v_out.dtype)
        v_out[:, (2 * h + 1) * LANES:(2 * h + 2) * LANES] = ones
    for h in range(IDX_HEADS):
        sl = slice(h * LANES, (h + 1) * LANES)
        qi_out[:, sl] = _rope(qi_ref[:, sl], ci, si, IDX_ROT_HALF).astype(qi_out.dtype)
    ki = ki_ref[...]
    ms = jnp.sum(ki * ki, axis=-1, keepdims=True) * (1.0 / IDX_DIM)
    ki = ki * lax.rsqrt(ms + NORM_EPS) * kgain_ref[...]
    ki_out[...] = _rope(ki, ci, si, IDX_ROT_HALF).astype(ki_out.dtype)
    wi_out[...] = misc_ref[...] * (IDX_HEADS ** -0.5 * IDX_DIM ** -0.5)


def _attn_prep(p, tabs, kgain, tm):
    m = p.shape[0]
    cosa, sina, cosi, sini = tabs

    def col(width, off):
        blk = off // width
        return pl.BlockSpec((tm, width), lambda i: (i, blk))

    row = pl.BlockSpec((tm, LANES), lambda i: (i, 0))
    outs = [(ATTN_HEADS * LANES, MXU_DTYPE), (ATTN_KV_HEADS * LANES, MXU_DTYPE),
            (2 * ATTN_KV_HEADS * LANES, MXU_DTYPE), (IDX_HEADS * LANES, MXU_DTYPE),
            (LANES, MXU_DTYPE), (LANES, F32)]
    return pl.pallas_call(
        _attn_prep_kernel,
        grid=(m // tm,),
        in_specs=[col(1024, COL_QA), col(256, COL_KA), col(256, COL_VA), col(1024, COL_QI),
                  col(128, COL_KI), col(128, COL_MISC), row, row, row, row,
                  pl.BlockSpec((1, LANES), lambda i: (0, 0))],
        out_specs=[pl.BlockSpec((tm, w), lambda i: (i, 0)) for w, _ in outs],
        out_shape=[jax.ShapeDtypeStruct((m, w), d) for w, d in outs],
        compiler_params=_params(("parallel",)),
        name="attn_prep",
    )(p, p, p, p, p, p, cosa, sina, cosi, sini, kgain)


def _nt_dot(a, b):
    return lax.dot_general(a, b, (((1,), (1,)), ((), ())), preferred_element_type=F32)


def _dsa_kernel(q_ref, qi_ref, wi_ref, k_ref, v_ref, kidx_ref, u_ref, o_ref,
                keys_ref, m_ref, acc_ref, *, tq, tk, topk):
    i = pl.program_id(1)
    nkt = (i * tq + tq + tk - 1) // tk
    row_t = i * tq + lax.broadcasted_iota(I32, (tq, 1), 0)

    qi = qi_ref[...]
    wi = wi_ref[...]

    def idx_body(kt, carry):
        off = pl.multiple_of(kt * tk, tk)
        kk = kidx_ref[pl.ds(off, tk), :]
        acc = jnp.zeros((tq, tk), F32)
        for h in range(IDX_HEADS):
            raw = _nt_dot(qi[:, h * LANES:(h + 1) * LANES], kk)
            acc = acc + jnp.maximum(raw, 0.0) * wi[:, MISC_WI + h:MISC_WI + h + 1]
        bits = pltpu.bitcast(acc, I32)
        key = jnp.where(bits < 0, (bits ^ jnp.int32(0x7FFFFFFF)) + 1, bits)
        col = off + lax.broadcasted_iota(I32, (1, tk), 1)
        keys_ref[kt] = jnp.where(col <= row_t, key, jnp.int32(KEY_MIN))
        return carry

    lax.fori_loop(0, nkt, idx_body, 0)

    def count_ge(cand):
        candb = jnp.broadcast_to(cand, (tq, LANES))

        def body(kt, acc):
            kk = keys_ref[kt]
            for c in range(tk // LANES):
                acc = acc + jnp.where(kk[:, c * LANES:(c + 1) * LANES] >= candb, 1.0, 0.0)
            return acc

        acc = lax.fori_loop(0, nkt, body, jnp.zeros((tq, LANES), F32))
        return jnp.sum(acc, axis=1, keepdims=True)

    kf = float(topk)

    def bits_cond(c):
        b, _, cnt = c
        return (b < 32) & (jnp.max(jnp.abs(cnt - kf)) > 0.0)

    def bits_body(c):
        b, ans, cnt = c
        cand = ans + lax.shift_left(jnp.int32(1), 31 - b)
        cc = count_ge(cand)
        take = cc >= kf
        return b + 1, jnp.where(take, cand, ans), jnp.where(take, cc, cnt)

    _, thr, cnt = lax.while_loop(
        bits_cond, bits_body,
        (jnp.int32(0), jnp.full((tq, 1), KEY_MIN, I32), jnp.full((tq, 1), nkt * tk, I32).astype(F32)))
    short = thr == jnp.int32(KEY_MIN)
    has_tie = jnp.max(jnp.where(short, 0.0, cnt - kf)) > 0.0

    q = q_ref[...]
    qg = [jnp.concatenate([q[:, (g * ATTN_GROUP + hh) * LANES:(g * ATTN_GROUP + hh + 1) * LANES]
                           for hh in range(ATTN_GROUP)], axis=0) for g in range(ATTN_KV_HEADS)]
    m_ref[...] = jnp.full(m_ref.shape, NEG_BIG, F32)
    acc_ref[...] = jnp.zeros(acc_ref.shape, F32)
    lane_tiles = tk // LANES

    def attend(kt, bias):
        off = pl.multiple_of(kt * tk, tk)
        s_kv = [_nt_dot(qg[g], k_ref[pl.ds(off, tk), g * LANES:(g + 1) * LANES])
                for g in range(ATTN_KV_HEADS)]
        for g in range(ATTN_KV_HEADS):
            ps, alphas = [], []
            for hh in range(ATTN_GROUP):
                h = g * ATTN_GROUP + hh
                s = s_kv[g][hh * tq:(hh + 1) * tq] + bias
                m_prev = m_ref[h]
                m_new = jnp.maximum(m_prev, jnp.max(s, axis=1, keepdims=True))
                m_ref[h] = m_new
                alphas.append(jnp.exp2(m_prev - m_new))
                ps.append(jnp.exp2(s - jnp.tile(m_new, (1, lane_tiles))).astype(MXU_DTYPE))
            v_t = v_ref[pl.ds(off, tk), 2 * g * LANES:(2 * g + 2) * LANES]
            pv = jnp.dot(jnp.concatenate(ps, axis=0), v_t, preferred_element_type=F32)
            for hh in range(ATTN_GROUP):
                h = g * ATTN_GROUP + hh
                acc_ref[h] = jnp.tile(alphas[hh], (1, 2)) * acc_ref[h] + pv[hh * tq:(hh + 1) * tq]

    thr_floor = jnp.maximum(thr, jnp.int32(KEY_MIN + 1))

    def plain_body(kt, carry):
        attend(kt, jnp.where(keys_ref[kt] >= thr_floor, 0.0, NEG_BIG))
        return carry

    def run_plain():
        lax.fori_loop(0, nkt, plain_body, 0)

    def run_ties():
        need = jnp.where(short, 0.0, kf - count_ge(thr + 1))

        def tie_body(kt, tie_carry):
            kk = keys_ref[kt]
            eq = kk == thr
            prefix = jnp.dot(jnp.where(eq, 1.0, 0.0).astype(BF16), u_ref[...],
                             preferred_element_type=F32)
            rank_ok = (prefix + tie_carry) <= need
            attend(kt, jnp.where(kk > thr, 0.0,
                                 jnp.where(eq, jnp.where(rank_ok, 0.0, NEG_BIG), NEG_BIG)))
            return tie_carry + prefix[:, tk - 1:tk]

        lax.fori_loop(0, nkt, tie_body, jnp.zeros((tq, 1), F32))

    lax.cond(has_tie, run_ties, run_plain)

    for h in range(ATTN_HEADS):
        acc = acc_ref[h]
        o_ref[:, h * LANES:(h + 1) * LANES] = (acc[:, :LANES] / acc[:, LANES:]).astype(o_ref.dtype)


def _dsa(q, k, v, qi, ki, wi, tie_u, tq, tk):
    b, s, _ = q.shape
    topk = min(INDEX_TOPK, s // 4)
    kern = functools.partial(_dsa_kernel, tq=tq, tk=tk, topk=topk)

    def qblk(w):
        return pl.BlockSpec((None, tq, w), lambda bi, i: (bi, i, 0))

    def full(w):
        return pl.BlockSpec((None, s, w), lambda bi, i: (bi, 0, 0))

    return pl.pallas_call(
        kern,
        grid=(b, s // tq),
        in_specs=[qblk(ATTN_HEADS * LANES), qblk(IDX_HEADS * LANES), qblk(LANES),
                  full(ATTN_KV_HEADS * LANES), full(2 * ATTN_KV_HEADS * LANES), full(LANES),
                  pl.BlockSpec((tk, tk), lambda bi, i: (0, 0))],
        out_specs=qblk(ATTN_HEADS * LANES),
        out_shape=jax.ShapeDtypeStruct((b, s, ATTN_HEADS * LANES), F32),
        scratch_shapes=[pltpu.VMEM((s // tk, tq, tk), I32),
                        pltpu.VMEM((ATTN_HEADS, tq, LANES), F32),
                        pltpu.VMEM((ATTN_HEADS, tq, 2 * LANES), F32)],
        compiler_params=_params(("parallel", "arbitrary")),
        name="dsa_attention",
    )(q, qi, wi, k, v, ki, tie_u)


def _shift_rows(x, halo, d):
    rolled = pltpu.roll(x, d, 0)
    head_rows = lax.broadcasted_iota(I32, (SUBLANES, x.shape[1]), 0)
    first = jnp.where(head_rows < d, pltpu.roll(halo, d, 0), rolled[:SUBLANES])
    return jnp.concatenate([first, rolled[SUBLANES:]], axis=0)


def _gdn_prep_kernel(q_ref, k_ref, v_ref, qh_ref, kh_ref, vh_ref, misc_ref,
                     wq_ref, wk_ref, wv_ref, alog_ref, dtb_ref,
                     qn_out, kn_out, vv_out, gcol_out, *, tm, seq):
    seq_start = (pl.program_id(0) * tm) % seq == 0

    def conv_silu(x_ref, halo_ref, w_ref):
        x = x_ref[...]
        halo = jnp.where(seq_start, 0.0, halo_ref[...])
        w = w_ref[...]
        y = x * w[CONV_K - 1:CONV_K]
        for d in range(1, CONV_K):
            y = y + _shift_rows(x, halo, d) * w[CONV_K - 1 - d:CONV_K - d]
        return y * jax.nn.sigmoid(y)

    def l2norm(x, out_ref, mult):
        for h in range(GDN_HEADS):
            sl = slice(h * LANES, (h + 1) * LANES)
            xh = x[:, sl]
            ss = jnp.sum(xh * xh, axis=-1, keepdims=True)
            out_ref[:, sl] = xh * (lax.rsqrt(ss + NORM_EPS) * mult)

    l2norm(conv_silu(q_ref, qh_ref, wq_ref), qn_out, GDN_DK ** -0.5)
    l2norm(conv_silu(k_ref, kh_ref, wk_ref), kn_out, 1.0)
    vv_out[...] = conv_silu(v_ref, vh_ref, wv_ref)

    misc = misc_ref[...]
    lane = lax.broadcasted_iota(I32, misc.shape, 1)
    row = lax.broadcasted_iota(I32, misc.shape, 0)
    is_a = (lane >= MISC_A) & (lane < MISC_A + GDN_HEADS)
    z = misc + dtb_ref[...]
    softplus = jnp.maximum(z, 0.0) + jnp.log(1.0 + jnp.exp(-jnp.abs(z)))
    g = jnp.where(is_a, -jnp.exp(alog_ref[...]) * softplus, 0.0)
    sh = 1
    while sh < CHUNK:
        g = g + jnp.where((row % CHUNK) >= sh, pltpu.roll(g, sh, 0), 0.0)
        sh *= 2
    is_beta = (lane >= MISC_BETA) & (lane < MISC_BETA + GDN_HEADS)
    gcol_out[...] = jnp.where(is_beta, jax.nn.sigmoid(misc), g)


def _gdn_prep(p, conv_w, alog_pad, dtb_pad, tm, seq):
    m = p.shape[0]
    wide = GDN_HEADS * LANES
    halo_blocks = tm // SUBLANES

    def col(off):
        blk = off // wide
        return pl.BlockSpec((tm, wide), lambda i: (i, blk))

    def halo(off):
        blk = off // wide
        return pl.BlockSpec((SUBLANES, wide), lambda i: (jnp.maximum(i * halo_blocks - 1, 0), blk))

    def wblk(j):
        return pl.BlockSpec((CONV_K, wide), lambda i: (0, j))

    one = pl.BlockSpec((1, LANES), lambda i: (0, 0))
    big = jax.ShapeDtypeStruct((m, wide), F32)
    kern = functools.partial(_gdn_prep_kernel, tm=tm, seq=seq)
    return pl.pallas_call(
        kern,
        grid=(m // tm,),
        in_specs=[col(COL_QB), col(COL_KB), col(COL_VB), halo(COL_QB), halo(COL_KB), halo(COL_VB),
                  pl.BlockSpec((tm, LANES), lambda i: (i, COL_MISC // LANES)),
                  wblk(0), wblk(1), wblk(2), one, one],
        out_specs=[pl.BlockSpec((tm, wide), lambda i: (i, 0))] * 3
        + [pl.BlockSpec((tm, LANES), lambda i: (i, 0))],
        out_shape=[big, big, big, jax.ShapeDtypeStruct((m, LANES), F32)],
        compiler_params=_params(("parallel",)),
        name="gdn_prep",
    )(p, p, p, p, p, p, p, conv_w, conv_w, conv_w, alog_pad, dtb_pad)


def _split(a):
    hi = a.astype(BF16)
    return hi, (a - hi.astype(F32)).astype(BF16)


def _dot3(a_split, b_split):
    (ah, al), (bh, bl) = a_split, b_split
    m = ah.shape[0]
    both = jnp.dot(jnp.concatenate([ah, al], axis=0), bh, preferred_element_type=F32)
    return both[:m] + both[m:] + jnp.dot(ah, bl, preferred_element_type=F32)


def _mxu_dot(a, b, dims=(((1,), (0,)), ((), ()))):
    return lax.dot_general(a.astype(MXU_DTYPE), b.astype(MXU_DTYPE), dims,
                           preferred_element_type=F32)


def _gdn_kernel(q_ref, k_ref, v_ref, gcol_ref, o_ref, state_ref):
    h = pl.program_id(0)

    @pl.when(pl.program_id(1) == 0)
    def _():
        state_ref[...] = jnp.zeros(state_ref.shape, F32)

    nb, rows = q_ref.shape[0], q_ref.shape[1]
    ii = lax.broadcasted_iota(I32, (CHUNK, CHUNK), 0)
    jj = lax.broadcasted_iota(I32, (CHUNK, CHUNK), 1)
    incl = ii >= jj
    strict = ii > jj
    eye = jnp.where(ii == jj, 1.0, 0.0)
    nt = (((1,), (1,)), ((), ()))
    tn = (((0,), (0,)), ((), ()))

    nsub = rows // CHUNK
    chains = [(b, sub) for b in range(nb) for sub in range(nsub)]
    col = {}
    for b in range(nb):
        gcol = gcol_ref[b]
        lane = lax.broadcasted_iota(I32, gcol.shape, 1)
        beta_all = jnp.sum(jnp.where(lane == MISC_BETA + h, gcol, 0.0), axis=1, keepdims=True)
        gc_all = jnp.sum(jnp.where(lane == MISC_A + h, gcol, 0.0), axis=1, keepdims=True)
        gc_sq = jnp.broadcast_to(gc_all, (rows, rows))
        col[b] = (beta_all, gc_all, gc_sq, gc_sq.T)

    xs, tinvs, rest = [], [], []
    for b, sub in chains:
        sl = slice(sub * CHUNK, (sub + 1) * CHUNK)
        beta_all, gc_all, gc_sq, gcr_sq = col[b]
        q, k, v = q_ref[b, sl, :], k_ref[b, sl, :], v_ref[b, sl, :]
        beta, gc = beta_all[sl], gc_all[sl]
        decay = jnp.where(incl, jnp.exp(gc_sq[sl, sl] - gcr_sq[sl, sl]), 0.0)
        kb = k * beta
        prods = _mxu_dot(jnp.concatenate([kb, q], axis=0), k, nt)
        x = -jnp.where(strict, prods[:CHUNK] * decay, 0.0)
        egc = jnp.exp(gc)
        g_last = gc[CHUNK - 1:CHUNK]
        xs.append(x)
        tinvs.append(eye + x)
        rest.append((jnp.concatenate([v * beta, kb * egc], axis=1), q * egc, prods[CHUNK:] * decay,
                     k * jnp.exp(g_last - gc), jnp.exp(g_last)))
    span = 1
    xsp = [_split(x) for x in xs]
    while 2 * span < CHUNK:
        xsp = [_split(_dot3(s, s)) for s in xsp]
        tinvs = [t + _dot3(_split(t), s) for t, s in zip(tinvs, xsp)]
        span *= 2
    uws = [_dot3(_split(t), _split(r[0])) for t, r in zip(tinvs, rest)]

    states = [state_ref[b] for b in range(nb)]
    for sub in range(nsub):
        sl = slice(sub * CHUNK, (sub + 1) * CHUNK)
        idx = [chains.index((b, sub)) for b in range(nb)]
        wss = [_mxu_dot(jnp.concatenate([uws[c][:, LANES:], rest[c][1]], axis=0), states[b])
               for b, c in enumerate(idx)]
        v_news = [uws[c][:, :LANES] - ws[:CHUNK] for c, ws in zip(idx, wss)]
        for b, c in enumerate(idx):
            o_ref[b, sl, :] = wss[b][CHUNK:] + _mxu_dot(rest[c][2], v_news[b])
        states = [states[b] * rest[c][4] + _mxu_dot(rest[c][3], v_news[b], tn)
                  for b, c in enumerate(idx)]
    for b in range(nb):
        state_ref[b] = states[b]


def _gdn(qn, kn, vv, gcol, rows):
    batch, seq, _ = qn.shape
    blk = pl.BlockSpec((batch, rows, LANES), lambda h, c: (0, c, h))
    return pl.pallas_call(
        _gdn_kernel,
        grid=(GDN_HEADS, seq // rows),
        in_specs=[blk, blk, blk, pl.BlockSpec((batch, rows, LANES), lambda h, c: (0, c, 0))],
        out_specs=blk,
        out_shape=jax.ShapeDtypeStruct(qn.shape, F32),
        scratch_shapes=[pltpu.VMEM((batch, GDN_DK, LANES), F32)],
        compiler_params=_params(("parallel", "arbitrary")),
        name="gdn_chunks",
    )(qn, kn, vv, gcol)


def _merge_kernel(oa_ref, ob_ref, za_ref, zb_ref, ga_ref, gb_ref, x_ref, gng_ref, w_ref, fg_ref,
                  o_ref, mix_ref, *, final_norm):
    za = za_ref[...]
    ya = oa_ref[...] * (za * jax.nn.sigmoid(za))
    mix_ref[...] = (jax.nn.sigmoid(ga_ref[...]) * ya).astype(mix_ref.dtype)
    gng = gng_ref[...]
    for h in range(GDN_HEADS):
        sl = slice(h * LANES, (h + 1) * LANES)
        ob = ob_ref[:, sl]
        zb = zb_ref[:, sl]
        ms = jnp.mean(ob * ob, axis=-1, keepdims=True)
        yb = ob * lax.rsqrt(ms + NORM_EPS) * gng * (zb * jax.nn.sigmoid(zb))
        mix_ref[:, sl] += jax.nn.sigmoid(gb_ref[:, sl]) * yb
    y = x_ref[...] + jnp.dot(mix_ref[...].astype(MXU_DTYPE), w_ref[...], preferred_element_type=F32)
    if final_norm:
        ms = jnp.mean(y * y, axis=-1, keepdims=True)
        y = y * lax.rsqrt(ms + NORM_EPS) * fg_ref[...]
    o_ref[...] = y


def _merge_out(oa, ob, p, x2d, gdn_gain, w_out, final_gain, tm, final_norm):
    m = x2d.shape[0]

    def col(off):
        blk = off // D_MODEL
        return pl.BlockSpec((tm, D_MODEL), lambda i: (i, blk))

    row = pl.BlockSpec((tm, D_MODEL), lambda i: (i, 0))
    kern = functools.partial(_merge_kernel, final_norm=final_norm)
    return pl.pallas_call(
        kern,
        grid=(m // tm,),
        in_specs=[row, row, col(COL_ZA), col(COL_ZB), col(COL_GA), col(COL_GB), row,
                  pl.BlockSpec((1, LANES), lambda i: (0, 0)),
                  pl.BlockSpec((D_MODEL, D_MODEL), lambda i: (0, 0)),
                  pl.BlockSpec((1, D_MODEL), lambda i: (0, 0))],
        out_specs=row,
        out_shape=jax.ShapeDtypeStruct((m, D_MODEL), F32),
        scratch_shapes=[pltpu.VMEM((tm, D_MODEL), F32)],
        compiler_params=_params(("parallel",)),
        name="merge_out_proj",
    )(oa, ob, p, p, p, p, x2d, gdn_gain, w_out, final_gain)


def _pack_w_in(w):
    edges = [0]
    for n in IN_SPLITS:
        edges.append(edges[-1] + n)
    (qa, ka, va, za, qi, ki, wi, qkvb, zb, beta, a, gates) = [
        w[:, edges[j]:edges[j + 1]] for j in range(len(IN_SPLITS))]
    d = w.shape[0]
    qi = jnp.pad(qi.reshape(d, IDX_HEADS, IDX_DIM), ((0, 0), (0, 0), (0, LANES - IDX_DIM)))
    qi = qi.reshape(d, IDX_HEADS * LANES)
    ki = jnp.pad(ki, ((0, 0), (0, LANES - IDX_DIM)))
    misc = jnp.pad(jnp.concatenate([wi, beta, a], axis=1), ((0, 0), (0, LANES - 3 * 8)))
    packed = jnp.concatenate(
        [qa, za, zb, gates[:, :D_MODEL], gates[:, D_MODEL:], qkvb, qi, ka, va, ki, misc], axis=1)
    return packed.astype(MXU_DTYPE)


def _lane_pad(v, off):
    return jnp.pad(v.astype(F32), (off, LANES - off - v.shape[0])).reshape(1, LANES)


def _pick(n, prefs):
    for t in prefs:
        if n % t == 0:
            return t
    raise ValueError(f"no tile for {n}")


def kernel(x, positions, norm_gain, w_in, conv_w, a_log, dt_bias, gdn_norm_gain, idx_k_gain, w_out,
           final_gain):
    batch, seq, d = x.shape
    assert d == D_MODEL and seq % 128 == 0
    depth = w_in.shape[0]
    m = batch * seq
    tm_proj = _pick(m, (1024, 512, 256, 128))
    tn_proj = 768
    tm_prep = _pick(seq, (512, 256, 128))
    tm_merge = _pick(m, (256, 128))
    tq = 128
    tk = _pick(seq, (512, 256, 128))

    lane = jnp.arange(LANES)

    def inv_freq(half):
        rot = 2 * half
        f = ROPE_THETA ** (-jnp.arange(0, rot, 2, dtype=F32) / rot)
        return jnp.where(lane < rot, f[lane % half], 0.0).reshape(1, LANES).astype(F32)

    tabs = _rope_tables(positions.reshape(m, 1).astype(I32), inv_freq(ATTN_ROT_HALF),
                        inv_freq(IDX_ROT_HALF), tm_prep)
    tie_u = (jnp.arange(tk)[:, None] <= jnp.arange(tk)[None, :]).astype(BF16)

    x2d = x.reshape(m, d)
    for layer in range(depth):
        p = _project(x2d, norm_gain[layer].reshape(1, d), _pack_w_in(w_in[layer]), tm_proj, tn_proj)

        kgain = _lane_pad(idx_k_gain[layer], 0)
        q, k, v, qi, ki, wi = _attn_prep(p, tabs, kgain, tm_prep)
        shp = lambda t: t.reshape(batch, seq, t.shape[-1])
        o_a = _dsa(shp(q), shp(k), shp(v), shp(qi), shp(ki), shp(wi), tie_u, tq, tk).reshape(m, d)

        qn, kn, vv, gcol = _gdn_prep(p, conv_w[layer], _lane_pad(a_log[layer], MISC_A),
                                     _lane_pad(dt_bias[layer], MISC_A), tm_prep, seq)
        o_b = _gdn(shp(qn), shp(kn), shp(vv), shp(gcol), 4 * CHUNK).reshape(m, d)

        x2d = _merge_out(o_a, o_b, p, x2d, gdn_norm_gain[layer].reshape(1, LANES),
                         w_out[layer].astype(MXU_DTYPE), final_gain.reshape(1, d), tm_merge,
                         final_norm=(layer == depth - 1))
    return x2d.reshape(batch, seq, d)
```

```python
import functools
import math

import jax
import jax.numpy as jnp
from jax import lax
from jax.experimental import pallas as pl
from jax.experimental.pallas import tpu as pltpu

F32 = jnp.float32
BF16 = jnp.bfloat16
I32 = jnp.int32

D_MODEL = 1024
ATTN_HEADS = 8
ATTN_KV_HEADS = 2
HEAD_DIM = 128
ATTN_GROUP = ATTN_HEADS // ATTN_KV_HEADS
ATTN_ROT_HALF = 16
IDX_HEADS = 8
IDX_DIM = 64
IDX_ROT_HALF = 8
INDEX_TOPK = 256
ROPE_THETA = 500000.0
GDN_HEADS = 8
GDN_DK = 128
CONV_K = 4
CHUNK = 64
NORM_EPS = 1e-6
IN_SPLITS = (1024, 256, 256, 1024, 512, 64, 8, 3072, 1024, 8, 8, 2048)

LANES = 128
SUBLANES = 8
VMEM_LIMIT = 56 * 1024 * 1024

COL_QA, COL_ZA, COL_ZB, COL_GA, COL_GB = 0, 1024, 2048, 3072, 4096
COL_QB, COL_KB, COL_VB, COL_QI = 5120, 6144, 7168, 8192
COL_KA, COL_VA, COL_KI, COL_MISC = 9216, 9472, 9728, 9856
P_COLS = 9984
MISC_WI, MISC_BETA, MISC_A = 0, 8, 16

MXU_DTYPE = BF16
GDN_PRECISION = lax.Precision.HIGHEST

KEY_MIN = -2 ** 31
NEG_BIG = -1e30
QK_SCALE_LOG2 = HEAD_DIM ** -0.5 * math.log2(math.e)
COUNT_ROWS = 128


def _params(sem):
    return pltpu.CompilerParams(dimension_semantics=sem, vmem_limit_bytes=VMEM_LIMIT)


def _proj_kernel(x_ref, gain_ref, w_ref, o_ref, h_ref):
    @pl.when(pl.program_id(1) == 0)
    def _():
        x = x_ref[...]
        ms = jnp.mean(x * x, axis=-1, keepdims=True)
        h_ref[...] = (x * lax.rsqrt(ms + NORM_EPS) * gain_ref[...]).astype(h_ref.dtype)

    o_ref[...] = jnp.dot(h_ref[...], w_ref[...], preferred_element_type=F32)


def _project(x2d, gain, w_packed, tm, tn):
    m = x2d.shape[0]
    return pl.pallas_call(
        _proj_kernel,
        grid=(m // tm, P_COLS // tn),
        in_specs=[
            pl.BlockSpec((tm, D_MODEL), lambda i, j: (i, 0)),
            pl.BlockSpec((1, D_MODEL), lambda i, j: (0, 0)),
            pl.BlockSpec((D_MODEL, tn), lambda i, j: (0, j)),
        ],
        out_specs=pl.BlockSpec((tm, tn), lambda i, j: (i, j)),
        out_shape=jax.ShapeDtypeStruct((m, P_COLS), F32),
        scratch_shapes=[pltpu.VMEM((tm, D_MODEL), MXU_DTYPE)],
        compiler_params=_params(("parallel", "arbitrary")),
        name="in_proj",
    )(x2d, gain, w_packed)


def _rope_tab_kernel(pos_ref, inva_ref, invi_ref, cosa_ref, sina_ref, cosi_ref, sini_ref):
    pos = pos_ref[...].astype(F32)
    lane = lax.broadcasted_iota(I32, (1, LANES), 1)

    def tables(inv, half, cos_ref, sin_ref):
        ang = pos * inv
        c = jnp.cos(ang)
        s = jnp.sin(ang)
        cos_ref[...] = jnp.where(lane < 2 * half, c, 1.0)
        sin_ref[...] = jnp.where(lane < half, -s, jnp.where(lane < 2 * half, s, 0.0))

    tables(inva_ref[...], ATTN_ROT_HALF, cosa_ref, sina_ref)
    tables(invi_ref[...], IDX_ROT_HALF, cosi_ref, sini_ref)


def _rope_tables(pos2d, inv_a, inv_i, tm):
    m = pos2d.shape[0]
    tab = jax.ShapeDtypeStruct((m, LANES), F32)
    row = pl.BlockSpec((tm, LANES), lambda i: (i, 0))
    one = pl.BlockSpec((1, LANES), lambda i: (0, 0))
    return pl.pallas_call(
        _rope_tab_kernel,
        grid=(m // tm,),
        in_specs=[pl.BlockSpec((tm, 1), lambda i: (i, 0)), one, one],
        out_specs=[row, row, row, row],
        out_shape=[tab, tab, tab, tab],
        compiler_params=_params(("parallel",)),
        name="rope_tables",
    )(pos2d, inv_a, inv_i)


def _rope(x, c, s, half):
    lane = lax.broadcasted_iota(I32, x.shape, 1)
    partner = jnp.where(lane < half, pltpu.roll(x, LANES - half, 1), pltpu.roll(x, half, 1))
    return x * c + partner * s


def _attn_prep_kernel(qa_ref, ka_ref, va_ref, qi_ref, ki_ref, misc_ref,
                      cosa_ref, sina_ref, cosi_ref, sini_ref, kgain_ref,
                      q_out, k_out, v_out, qi_out, ki_out, wi_out):
    ca, sa = cosa_ref[...], sina_ref[...]
    ci, si = cosi_ref[...], sini_ref[...]
    for h in range(ATTN_HEADS):
        sl = slice(h * LANES, (h + 1) * LANES)
        q_out[:, sl] = (_rope(qa_ref[:, sl], ca, sa, ATTN_ROT_HALF) * QK_SCALE_LOG2).astype(q_out.dtype)
    ones = jnp.ones((va_ref.shape[0], LANES), v_out.dtype)
    for h in range(ATTN_KV_HEADS):
        sl = slice(h * LANES, (h + 1) * LANES)
        k_out[:, sl] = _rope(ka_ref[:, sl], ca, sa, ATTN_ROT_HALF).astype(k_out.dtype)
        v_out[:, 2 * h * LANES:(2 * h + 1) * LANES] = va_ref[:, sl].astype(v_out.dtype)
        v_out[:, (2 * h + 1) * LANES:(2 * h + 2) * LANES] = ones
    for h in range(IDX_HEADS):
        sl = slice(h * LANES, (h + 1) * LANES)
        qi_out[:, sl] = _rope(qi_ref[:, sl], ci, si, IDX_ROT_HALF).astype(qi_out.dtype)
    ki = ki_ref[...]
    ms = jnp.sum(ki * ki, axis=-1, keepdims=True) * (1.0 / IDX_DIM)
    ki = ki * lax.rsqrt(ms + NORM_EPS) * kgain_ref[...]
    ki_out[...] = _rope(ki, ci, si, IDX_ROT_HALF).astype(ki_out.dtype)
    wi_out[...] = misc_ref[...] * (IDX_HEADS ** -0.5 * IDX_DIM ** -0.5)


def _attn_prep(p, tabs, kgain, tm):
    m = p.shape[0]
    cosa, sina, cosi, sini = tabs

    def col(width, off):
        blk = off // width
        return pl.BlockSpec((tm, width), lambda i: (i, blk))

    row = pl.BlockSpec((tm, LANES), lambda i: (i, 0))
    outs = [(ATTN_HEADS * LANES, MXU_DTYPE), (ATTN_KV_HEADS * LANES, MXU_DTYPE),
            (2 * ATTN_KV_HEADS * LANES, MXU_DTYPE), (IDX_HEADS * LANES, MXU_DTYPE),
            (LANES, MXU_DTYPE), (LANES, F32)]
    return pl.pallas_call(
        _attn_prep_kernel,
        grid=(m // tm,),
        in_specs=[col(1024, COL_QA), col(256, COL_KA), col(256, COL_VA), col(1024, COL_QI),
                  col(128, COL_KI), col(128, COL_MISC), row, row, row, row,
                  pl.BlockSpec((1, LANES), lambda i: (0, 0))],
        out_specs=[pl.BlockSpec((tm, w), lambda i: (i, 0)) for w, _ in outs],
        out_shape=[jax.ShapeDtypeStruct((m, w), d) for w, d in outs],
        compiler_params=_params(("parallel",)),
        name="attn_prep",
    )(p, p, p, p, p, p, cosa, sina, cosi, sini, kgain)


def _nt_dot(a, b):
    return lax.dot_general(a, b, (((1,), (1,)), ((), ())), preferred_element_type=F32)


def _dsa_kernel(q_ref, qi_ref, wi_ref, k_ref, v_ref, kidx_ref, u_ref, o_ref,
                keys_ref, m_ref, acc_ref, *, tq, tk, topk):
    i = pl.program_id(1)
    nkt = (i * tq + tq + tk - 1) // tk
    row_t = i * tq + lax.broadcasted_iota(I32, (tq, 1), 0)

    qi = qi_ref[...]
    wi = wi_ref[...]

    wb = [jnp.tile(jnp.broadcast_to(wi[:, MISC_WI + h:MISC_WI + h + 1], (tq, LANES)),
                   (1, tk // LANES)) for h in range(IDX_HEADS)]

    def index_tiles(kts):
        offs = [pl.multiple_of(kt * tk, tk) for kt in kts]
        raws = [[_nt_dot(qi[:, h * LANES:(h + 1) * LANES], kidx_ref[pl.ds(off, tk), :])
                 for h in range(IDX_HEADS)] for off in offs]
        for kt, off, raw in zip(kts, offs, raws):
            acc = jnp.maximum(raw[0], 0.0) * wb[0]
            for h in range(1, IDX_HEADS):
                acc = acc + jnp.maximum(raw[h], 0.0) * wb[h]
            bits = pltpu.bitcast(acc, I32)
            key = jnp.where(bits < 0, (bits ^ jnp.int32(0x7FFFFFFF)) + 1, bits)
            col = off + lax.broadcasted_iota(I32, (1, tk), 1)
            keys_ref[kt] = jnp.where(col <= row_t, key, jnp.int32(KEY_MIN))

    def idx_pair(j, carry):
        index_tiles([2 * j, 2 * j + 1])
        return carry

    lax.fori_loop(0, nkt // 2, idx_pair, 0)

    @pl.when(nkt % 2 == 1)
    def _():
        index_tiles([nkt - 1])

    def count_ge(cand):
        parts = []
        for r0 in range(0, tq, COUNT_ROWS):
            candb = jnp.broadcast_to(cand[r0:r0 + COUNT_ROWS], (COUNT_ROWS, LANES))

            def body(kt, acc, r0=r0, candb=candb):
                for c in range(tk // LANES):
                    kk = keys_ref[kt, r0:r0 + COUNT_ROWS, c * LANES:(c + 1) * LANES]
                    acc = acc + jnp.where(kk >= candb, 1.0, 0.0)
                return acc

            acc = lax.fori_loop(0, nkt, body, jnp.zeros((COUNT_ROWS, LANES), F32))
            parts.append(jnp.sum(acc, axis=1, keepdims=True))
        return jnp.concatenate(parts, axis=0)

    kf = float(topk)

    def bits_cond(c):
        b, _, cnt = c
        return (b < 32) & (jnp.max(jnp.abs(cnt - kf)) > 0.0)

    def bits_body(c):
        b, ans, cnt = c
        cand = ans + lax.shift_left(jnp.int32(1), 31 - b)
        cc = count_ge(cand)
        take = cc >= kf
        return b + 1, jnp.where(take, cand, ans), jnp.where(take, cc, cnt)

    _, thr, cnt = lax.while_loop(
        bits_cond, bits_body,
        (jnp.int32(0), jnp.full((tq, 1), KEY_MIN, I32), jnp.full((tq, 1), nkt * tk, I32).astype(F32)))
    short = thr == jnp.int32(KEY_MIN)
    has_tie = jnp.max(jnp.where(short, 0.0, cnt - kf)) > 0.0

    q = q_ref[...]
    qg = [jnp.concatenate([q[:, (g * ATTN_GROUP + hh) * LANES:(g * ATTN_GROUP + hh + 1) * LANES]
                           for hh in range(ATTN_GROUP)], axis=0) for g in range(ATTN_KV_HEADS)]
    m_ref[...] = jnp.full(m_ref.shape, NEG_BIG, F32)
    acc_ref[...] = jnp.zeros(acc_ref.shape, F32)
    lane_tiles = tk // LANES

    def attend(tiles):
        offs = [pl.multiple_of(kt * tk, tk) for kt, _ in tiles]
        s_all = [[_nt_dot(qg[g], k_ref[pl.ds(off, tk), g * LANES:(g + 1) * LANES])
                  for g in range(ATTN_KV_HEADS)] for off in offs]
        for (_, bias), off, s_kv in zip(tiles, offs, s_all):
            for g in range(ATTN_KV_HEADS):
                ps, alphas = [], []
                for hh in range(ATTN_GROUP):
                    h = g * ATTN_GROUP + hh
                    s = s_kv[g][hh * tq:(hh + 1) * tq] + bias
                    m_prev = m_ref[h]
                    m_new = jnp.maximum(m_prev, jnp.max(s, axis=1, keepdims=True))
                    m_ref[h] = m_new
                    alphas.append(jnp.exp2(m_prev - m_new))
                    ps.append(jnp.exp2(s - jnp.tile(m_new, (1, lane_tiles))).astype(MXU_DTYPE))
                v_t = v_ref[pl.ds(off, tk), 2 * g * LANES:(2 * g + 2) * LANES]
                pv = jnp.dot(jnp.concatenate(ps, axis=0), v_t, preferred_element_type=F32)
                for hh in range(ATTN_GROUP):
                    h = g * ATTN_GROUP + hh
                    acc_ref[h] = (jnp.tile(alphas[hh], (1, 2)) * acc_ref[h]
                                  + pv[hh * tq:(hh + 1) * tq])

    thr_floor = jnp.maximum(thr, jnp.int32(KEY_MIN + 1))

    def plain_bias(kt):
        return jnp.where(keys_ref[kt] >= thr_floor, 0.0, NEG_BIG)

    def pair_body(j, carry):
        attend([(2 * j, plain_bias(2 * j)), (2 * j + 1, plain_bias(2 * j + 1))])
        return carry

    def run_plain():
        lax.fori_loop(0, nkt // 2, pair_body, 0)

        @pl.when(nkt % 2 == 1)
        def _():
            attend([(nkt - 1, plain_bias(nkt - 1))])

    def run_ties():
        need = jnp.where(short, 0.0, kf - count_ge(thr + 1))

        def tie_body(kt, tie_carry):
            kk = keys_ref[kt]
            eq = kk == thr
            prefix = jnp.dot(jnp.where(eq, 1.0, 0.0).astype(BF16), u_ref[...],
                             preferred_element_type=F32)
            rank_ok = (prefix + tie_carry) <= need
            attend([(kt, jnp.where(kk > thr, 0.0,
                                   jnp.where(eq, jnp.where(rank_ok, 0.0, NEG_BIG), NEG_BIG)))])
            return tie_carry + prefix[:, tk - 1:tk]

        lax.fori_loop(0, nkt, tie_body, jnp.zeros((tq, 1), F32))

    lax.cond(has_tie, run_ties, run_plain)

    for h in range(ATTN_HEADS):
        acc = acc_ref[h]
        o_ref[:, h * LANES:(h + 1) * LANES] = (acc[:, :LANES] / acc[:, LANES:]).astype(o_ref.dtype)


def _dsa(q, k, v, qi, ki, wi, tie_u, tq, tk):
    b, s, _ = q.shape
    topk = min(INDEX_TOPK, s // 4)
    kern = functools.partial(_dsa_kernel, tq=tq, tk=tk, topk=topk)

    def qblk(w):
        return pl.BlockSpec((None, tq, w), lambda bi, i: (bi, i, 0))

    def full(w):
        return pl.BlockSpec((None, s, w), lambda bi, i: (bi, 0, 0))

    return pl.pallas_call(
        kern,
        grid=(b, s // tq),
        in_specs=[qblk(ATTN_HEADS * LANES), qblk(IDX_HEADS * LANES), qblk(LANES),
                  full(ATTN_KV_HEADS * LANES), full(2 * ATTN_KV_HEADS * LANES), full(LANES),
                  pl.BlockSpec((tk, tk), lambda bi, i: (0, 0))],
        out_specs=qblk(ATTN_HEADS * LANES),
        out_shape=jax.ShapeDtypeStruct((b, s, ATTN_HEADS * LANES), F32),
        scratch_shapes=[pltpu.VMEM((s // tk, tq, tk), I32),
                        pltpu.VMEM((ATTN_HEADS, tq, LANES), F32),
                        pltpu.VMEM((ATTN_HEADS, tq, 2 * LANES), F32)],
        compiler_params=_params(("parallel", "arbitrary")),
        name="dsa_attention",
    )(q, qi, wi, k, v, ki, tie_u)


def _shift_rows(x, halo, d):
    rolled = pltpu.roll(x, d, 0)
    head_rows = lax.broadcasted_iota(I32, (SUBLANES, x.shape[1]), 0)
    first = jnp.where(head_rows < d, pltpu.roll(halo, d, 0), rolled[:SUBLANES])
    return jnp.concatenate([first, rolled[SUBLANES:]], axis=0)


def _gdn_prep_kernel(q_ref, k_ref, v_ref, qh_ref, kh_ref, vh_ref, misc_ref,
                     wq_ref, wk_ref, wv_ref, alog_ref, dtb_ref,
                     qn_out, kn_out, vv_out, gcol_out, *, tm, seq):
    seq_start = (pl.program_id(0) * tm) % seq == 0

    def conv_silu(x_ref, halo_ref, w_ref):
        x = x_ref[...]
        halo = jnp.where(seq_start, 0.0, halo_ref[...])
        w = w_ref[...]
        y = x * w[CONV_K - 1:CONV_K]
        for d in range(1, CONV_K):
            y = y + _shift_rows(x, halo, d) * w[CONV_K - 1 - d:CONV_K - d]
        return y * jax.nn.sigmoid(y)

    def l2norm(x, out_ref, mult):
        for h in range(GDN_HEADS):
            sl = slice(h * LANES, (h + 1) * LANES)
            xh = x[:, sl]
            ss = jnp.sum(xh * xh, axis=-1, keepdims=True)
            out_ref[:, sl] = xh * (lax.rsqrt(ss + NORM_EPS) * mult)

    l2norm(conv_silu(q_ref, qh_ref, wq_ref), qn_out, GDN_DK ** -0.5)
    l2norm(conv_silu(k_ref, kh_ref, wk_ref), kn_out, 1.0)
    vv_out[...] = conv_silu(v_ref, vh_ref, wv_ref)

    misc = misc_ref[...]
    lane = lax.broadcasted_iota(I32, misc.shape, 1)
    row = lax.broadcasted_iota(I32, misc.shape, 0)
    is_a = (lane >= MISC_A) & (lane < MISC_A + GDN_HEADS)
    z = misc + dtb_ref[...]
    softplus = jnp.maximum(z, 0.0) + jnp.log(1.0 + jnp.exp(-jnp.abs(z)))
    g = jnp.where(is_a, -jnp.exp(alog_ref[...]) * softplus, 0.0)
    sh = 1
    while sh < CHUNK:
        g = g + jnp.where((row % CHUNK) >= sh, pltpu.roll(g, sh, 0), 0.0)
        sh *= 2
    is_beta = (lane >= MISC_BETA) & (lane < MISC_BETA + GDN_HEADS)
    gcol_out[...] = jnp.where(is_beta, jax.nn.sigmoid(misc), g)


def _gdn_prep(p, conv_w, alog_pad, dtb_pad, tm, seq):
    m = p.shape[0]
    wide = GDN_HEADS * LANES
    halo_blocks = tm // SUBLANES

    def col(off):
        blk = off // wide
        return pl.BlockSpec((tm, wide), lambda i: (i, blk))

    def halo(off):
        blk = off // wide
        return pl.BlockSpec((SUBLANES, wide), lambda i: (jnp.maximum(i * halo_blocks - 1, 0), blk))

    def wblk(j):
        return pl.BlockSpec((CONV_K, wide), lambda i: (0, j))

    one = pl.BlockSpec((1, LANES), lambda i: (0, 0))
    big = jax.ShapeDtypeStruct((m, wide), F32)
    kern = functools.partial(_gdn_prep_kernel, tm=tm, seq=seq)
    return pl.pallas_call(
        kern,
        grid=(m // tm,),
        in_specs=[col(COL_QB), col(COL_KB), col(COL_VB), halo(COL_QB), halo(COL_KB), halo(COL_VB),
                  pl.BlockSpec((tm, LANES), lambda i: (i, COL_MISC // LANES)),
                  wblk(0), wblk(1), wblk(2), one, one],
        out_specs=[pl.BlockSpec((tm, wide), lambda i: (i, 0))] * 3
        + [pl.BlockSpec((tm, LANES), lambda i: (i, 0))],
        out_shape=[big, big, big, jax.ShapeDtypeStruct((m, LANES), F32)],
        compiler_params=_params(("parallel",)),
        name="gdn_prep",
    )(p, p, p, p, p, p, p, conv_w, conv_w, conv_w, alog_pad, dtb_pad)


def _split(a):
    hi = a.astype(BF16)
    return hi, (a - hi.astype(F32)).astype(BF16)


def _dot3(a_split, b_split):
    (ah, al), (bh, bl) = a_split, b_split
    m = ah.shape[0]
    both = jnp.dot(jnp.concatenate([ah, al], axis=0), bh, preferred_element_type=F32)
    return both[:m] + both[m:] + jnp.dot(ah, bl, preferred_element_type=F32)


def _mxu_dot(a, b, dims=(((1,), (0,)), ((), ()))):
    return lax.dot_general(a.astype(MXU_DTYPE), b.astype(MXU_DTYPE), dims,
                           preferred_element_type=F32)


def _gdn_kernel(q_ref, k_ref, v_ref, gcol_ref, o_ref, state_ref):
    h = pl.program_id(0)

    @pl.when(pl.program_id(1) == 0)
    def _():
        state_ref[...] = jnp.zeros(state_ref.shape, F32)

    nb, rows = q_ref.shape[0], q_ref.shape[1]
    ii = lax.broadcasted_iota(I32, (CHUNK, CHUNK), 0)
    jj = lax.broadcasted_iota(I32, (CHUNK, CHUNK), 1)
    incl = ii >= jj
    strict = ii > jj
    eye = jnp.where(ii == jj, 1.0, 0.0)
    nt = (((1,), (1,)), ((), ()))
    tn = (((0,), (0,)), ((), ()))

    nsub = rows // CHUNK
    chains = [(b, sub) for b in range(nb) for sub in range(nsub)]
    col = {}
    for b in range(nb):
        gcol = gcol_ref[b]
        lane = lax.broadcasted_iota(I32, gcol.shape, 1)
        beta_all = jnp.sum(jnp.where(lane == MISC_BETA + h, gcol, 0.0), axis=1, keepdims=True)
        gc_all = jnp.sum(jnp.where(lane == MISC_A + h, gcol, 0.0), axis=1, keepdims=True)
        gc_sq = jnp.broadcast_to(gc_all, (rows, rows))
        col[b] = (beta_all, gc_all, gc_sq, gc_sq.T)

    xs, tinvs, rest = [], [], []
    for b, sub in chains:
        sl = slice(sub * CHUNK, (sub + 1) * CHUNK)
        beta_all, gc_all, gc_sq, gcr_sq = col[b]
        q, k, v = q_ref[b, sl, :], k_ref[b, sl, :], v_ref[b, sl, :]
        beta, gc = beta_all[sl], gc_all[sl]
        decay = jnp.where(incl, jnp.exp(gc_sq[sl, sl] - gcr_sq[sl, sl]), 0.0)
        kb = k * beta
        prods = _mxu_dot(jnp.concatenate([kb, q], axis=0), k, nt)
        x = -jnp.where(strict, prods[:CHUNK] * decay, 0.0)
        egc = jnp.exp(gc)
        g_last = gc[CHUNK - 1:CHUNK]
        xs.append(x)
        tinvs.append(eye + x)
        rest.append((jnp.concatenate([v * beta, kb * egc], axis=1), q * egc, prods[CHUNK:] * decay,
                     k * jnp.exp(g_last - gc), jnp.exp(g_last)))
    span = 1
    xsp = [_split(x) for x in xs]
    while 2 * span < CHUNK:
        xsp = [_split(_dot3(s, s)) for s in xsp]
        tinvs = [t + _dot3(_split(t), s) for t, s in zip(tinvs, xsp)]
        span *= 2
    uws = [_dot3(_split(t), _split(r[0])) for t, r in zip(tinvs, rest)]

    states = [state_ref[b] for b in range(nb)]
    for sub in range(nsub):
        sl = slice(sub * CHUNK, (sub + 1) * CHUNK)
        idx = [chains.index((b, sub)) for b in range(nb)]
        wss = [_mxu_dot(jnp.concatenate([uws[c][:, LANES:], rest[c][1]], axis=0), states[b])
               for b, c in enumerate(idx)]
        v_news = [uws[c][:, :LANES] - ws[:CHUNK] for c, ws in zip(idx, wss)]
        for b, c in enumerate(idx):
            o_ref[b, sl, :] = wss[b][CHUNK:] + _mxu_dot(rest[c][2], v_news[b])
        states = [states[b] * rest[c][4] + _mxu_dot(rest[c][3], v_news[b], tn)
                  for b, c in enumerate(idx)]
    for b in range(nb):
        state_ref[b] = states[b]


def _gdn(qn, kn, vv, gcol, rows):
    batch, seq, _ = qn.shape
    blk = pl.BlockSpec((batch, rows, LANES), lambda h, c: (0, c, h))
    return pl.pallas_call(
        _gdn_kernel,
        grid=(GDN_HEADS, seq // rows),
        in_specs=[blk, blk, blk, pl.BlockSpec((batch, rows, LANES), lambda h, c: (0, c, 0))],
        out_specs=blk,
        out_shape=jax.ShapeDtypeStruct(qn.shape, F32),
        scratch_shapes=[pltpu.VMEM((batch, GDN_DK, LANES), F32)],
        compiler_params=_params(("parallel", "arbitrary")),
        name="gdn_chunks",
    )(qn, kn, vv, gcol)


def _merge_kernel(oa_ref, ob_ref, za_ref, zb_ref, ga_ref, gb_ref, x_ref, gng_ref, w_ref, fg_ref,
                  o_ref, mix_ref, *, final_norm):
    za = za_ref[...]
    ya = oa_ref[...] * (za * jax.nn.sigmoid(za))
    mix_ref[...] = (jax.nn.sigmoid(ga_ref[...]) * ya).astype(mix_ref.dtype)
    gng = gng_ref[...]
    for h in range(GDN_HEADS):
        sl = slice(h * LANES, (h + 1) * LANES)
        ob = ob_ref[:, sl]
        zb = zb_ref[:, sl]
        ms = jnp.mean(ob * ob, axis=-1, keepdims=True)
        yb = ob * lax.rsqrt(ms + NORM_EPS) * gng * (zb * jax.nn.sigmoid(zb))
        mix_ref[:, sl] += jax.nn.sigmoid(gb_ref[:, sl]) * yb
    y = x_ref[...] + jnp.dot(mix_ref[...].astype(MXU_DTYPE), w_ref[...], preferred_element_type=F32)
    if final_norm:
        ms = jnp.mean(y * y, axis=-1, keepdims=True)
        y = y * lax.rsqrt(ms + NORM_EPS) * fg_ref[...]
    o_ref[...] = y


def _merge_out(oa, ob, p, x2d, gdn_gain, w_out, final_gain, tm, final_norm):
    m = x2d.shape[0]

    def col(off):
        blk = off // D_MODEL
        return pl.BlockSpec((tm, D_MODEL), lambda i: (i, blk))

    row = pl.BlockSpec((tm, D_MODEL), lambda i: (i, 0))
    kern = functools.partial(_merge_kernel, final_norm=final_norm)
    return pl.pallas_call(
        kern,
        grid=(m // tm,),
        in_specs=[row, row, col(COL_ZA), col(COL_ZB), col(COL_GA), col(COL_GB), row,
                  pl.BlockSpec((1, LANES), lambda i: (0, 0)),
                  pl.BlockSpec((D_MODEL, D_MODEL), lambda i: (0, 0)),
                  pl.BlockSpec((1, D_MODEL), lambda i: (0, 0))],
        out_specs=row,
        out_shape=jax.ShapeDtypeStruct((m, D_MODEL), F32),
        scratch_shapes=[pltpu.VMEM((tm, D_MODEL), F32)],
        compiler_params=_params(("parallel",)),
        name="merge_out_proj",
    )(oa, ob, p, p, p, p, x2d, gdn_gain, w_out, final_gain)


def _pack_w_in(w):
    edges = [0]
    for n in IN_SPLITS:
        edges.append(edges[-1] + n)
    (qa, ka, va, za, qi, ki, wi, qkvb, zb, beta, a, gates) = [
        w[:, edges[j]:edges[j + 1]] for j in range(len(IN_SPLITS))]
    d = w.shape[0]
    qi = jnp.pad(qi.reshape(d, IDX_HEADS, IDX_DIM), ((0, 0), (0, 0), (0, LANES - IDX_DIM)))
    qi = qi.reshape(d, IDX_HEADS * LANES)
    ki = jnp.pad(ki, ((0, 0), (0, LANES - IDX_DIM)))
    misc = jnp.pad(jnp.concatenate([wi, beta, a], axis=1), ((0, 0), (0, LANES - 3 * 8)))
    packed = jnp.concatenate(
        [qa, za, zb, gates[:, :D_MODEL], gates[:, D_MODEL:], qkvb, qi, ka, va, ki, misc], axis=1)
    return packed.astype(MXU_DTYPE)


def _lane_pad(v, off):
    return jnp.pad(v.astype(F32), (off, LANES - off - v.shape[0])).reshape(1, LANES)


def _pick(n, prefs):
    for t in prefs:
        if n % t == 0:
            return t
    raise ValueError(f"no tile for {n}")


def kernel(x, positions, norm_gain, w_in, conv_w, a_log, dt_bias, gdn_norm_gain, idx_k_gain, w_out,
           final_gain):
    batch, seq, d = x.shape
    assert d == D_MODEL and seq % 128 == 0
    depth = w_in.shape[0]
    m = batch * seq
    tm_proj = _pick(m, (1024, 512, 256, 128))
    tn_proj = 768
    tm_prep = _pick(seq, (512, 256, 128))
    tm_merge = _pick(m, (256, 128))
    tq = 128
    tk = _pick(seq, (512, 256, 128))

    lane = jnp.arange(LANES)

    def inv_freq(half):
        rot = 2 * half
        f = ROPE_THETA ** (-jnp.arange(0, rot, 2, dtype=F32) / rot)
        return jnp.where(lane < rot, f[lane % half], 0.0).reshape(1, LANES).astype(F32)

    tabs = _rope_tables(positions.reshape(m, 1).astype(I32), inv_freq(ATTN_ROT_HALF),
                        inv_freq(IDX_ROT_HALF), tm_prep)
    tie_u = (jnp.arange(tk)[:, None] <= jnp.arange(tk)[None, :]).astype(BF16)

    x2d = x.reshape(m, d)
    for layer in range(depth):
        p = _project(x2d, norm_gain[layer].reshape(1, d), _pack_w_in(w_in[layer]), tm_proj, tn_proj)

        kgain = _lane_pad(idx_k_gain[layer], 0)
        q, k, v, qi, ki, wi = _attn_prep(p, tabs, kgain, tm_prep)
        shp = lambda t: t.reshape(batch, seq, t.shape[-1])
        o_a = _dsa(shp(q), shp(k), shp(v), shp(qi), shp(ki), shp(wi), tie_u, tq, tk).reshape(m, d)

        qn, kn, vv, gcol = _gdn_prep(p, conv_w[layer], _lane_pad(a_log[layer], MISC_A),
                                     _lane_pad(dt_bias[layer], MISC_A), tm_prep, seq)
        o_b = _gdn(shp(qn), shp(kn), shp(vv), shp(gcol), 8 * CHUNK).reshape(m, d)

        x2d = _merge_out(o_a, o_b, p, x2d, gdn_norm_gain[layer].reshape(1, LANES),
                         w_out[layer].astype(MXU_DTYPE), final_gain.reshape(1, d), tm_merge,
                         final_norm=(layer == depth - 1))
    return x2d.reshape(batch, seq, d)
```

```python
import functools
import math

import jax
import jax.numpy as jnp
from jax import lax
from jax.experimental import pallas as pl
from jax.experimental.pallas import tpu as pltpu

F32 = jnp.float32
BF16 = jnp.bfloat16
I32 = jnp.int32

D_MODEL = 1024
ATTN_HEADS = 8
ATTN_KV_HEADS = 2
HEAD_DIM = 128
ATTN_GROUP = ATTN_HEADS // ATTN_KV_HEADS
ATTN_ROT_HALF = 16
IDX_HEADS = 8
IDX_DIM = 64
IDX_ROT_HALF = 8
INDEX_TOPK = 256
ROPE_THETA = 500000.0
GDN_HEADS = 8
GDN_DK = 128
CONV_K = 4
CHUNK = 64
NORM_EPS = 1e-6
IN_SPLITS = (1024, 256, 256, 1024, 512, 64, 8, 3072, 1024, 8, 8, 2048)

LANES = 128
SUBLANES = 8
VMEM_LIMIT = 56 * 1024 * 1024

COL_QA, COL_ZA, COL_ZB, COL_GA, COL_GB = 0, 1024, 2048, 3072, 4096
COL_QB, COL_KB, COL_VB, COL_QI = 5120, 6144, 7168, 8192
COL_KA, COL_VA, COL_KI, COL_MISC = 9216, 9472, 9728, 9856
P_COLS = 9984
MISC_WI, MISC_BETA, MISC_A = 0, 8, 16

MXU_DTYPE = BF16
GDN_PRECISION = lax.Precision.HIGHEST

KEY_MIN = -2 ** 31
NEG_BIG = -1e30
QK_SCALE_LOG2 = HEAD_DIM ** -0.5 * math.log2(math.e)
COUNT_ROWS = 128
BITS_PER_CHECK = 4
GDN_PACK = 4


def _params(sem):
    return pltpu.CompilerParams(dimension_semantics=sem, vmem_limit_bytes=VMEM_LIMIT)


def _proj_kernel(x_ref, gain_ref, w_ref, o_ref, h_ref):
    @pl.when(pl.program_id(1) == 0)
    def _():
        x = x_ref[...]
        ms = jnp.mean(x * x, axis=-1, keepdims=True)
        h_ref[...] = (x * lax.rsqrt(ms + NORM_EPS) * gain_ref[...]).astype(h_ref.dtype)

    o_ref[...] = jnp.dot(h_ref[...], w_ref[...], preferred_element_type=F32)


def _project(x2d, gain, w_packed, tm, tn):
    m = x2d.shape[0]
    return pl.pallas_call(
        _proj_kernel,
        grid=(m // tm, P_COLS // tn),
        in_specs=[
            pl.BlockSpec((tm, D_MODEL), lambda i, j: (i, 0)),
            pl.BlockSpec((1, D_MODEL), lambda i, j: (0, 0)),
            pl.BlockSpec((D_MODEL, tn), lambda i, j: (0, j)),
        ],
        out_specs=pl.BlockSpec((tm, tn), lambda i, j: (i, j)),
        out_shape=jax.ShapeDtypeStruct((m, P_COLS), F32),
        scratch_shapes=[pltpu.VMEM((tm, D_MODEL), MXU_DTYPE)],
        compiler_params=_params(("parallel", "arbitrary")),
        name="in_proj",
    )(x2d, gain, w_packed)


def _rope_tab_kernel(pos_ref, inva_ref, invi_ref, cosa_ref, sina_ref, cosi_ref, sini_ref):
    pos = pos_ref[...].astype(F32)
    lane = lax.broadcasted_iota(I32, (1, LANES), 1)

    def tables(inv, half, cos_ref, sin_ref):
        ang = pos * inv
        c = jnp.cos(ang)
        s = jnp.sin(ang)
        cos_ref[...] = jnp.where(lane < 2 * half, c, 1.0)
        sin_ref[...] = jnp.where(lane < half, -s, jnp.where(lane < 2 * half, s, 0.0))

    tables(inva_ref[...], ATTN_ROT_HALF, cosa_ref, sina_ref)
    tables(invi_ref[...], IDX_ROT_HALF, cosi_ref, sini_ref)


def _rope_tables(pos2d, inv_a, inv_i, tm):
    m = pos2d.shape[0]
    tab = jax.ShapeDtypeStruct((m, LANES), F32)
    row = pl.BlockSpec((tm, LANES), lambda i: (i, 0))
    one = pl.BlockSpec((1, LANES), lambda i: (0, 0))
    return pl.pallas_call(
        _rope_tab_kernel,
        grid=(m // tm,),
        in_specs=[pl.BlockSpec((tm, 1), lambda i: (i, 0)), one, one],
        out_specs=[row, row, row, row],
        out_shape=[tab, tab, tab, tab],
        compiler_params=_params(("parallel",)),
        name="rope_tables",
    )(pos2d, inv_a, inv_i)


def _rope(x, c, s, half):
    lane = lax.broadcasted_iota(I32, x.shape, 1)
    partner = jnp.where(lane < half, pltpu.roll(x, LANES - half, 1), pltpu.roll(x, half, 1))
    return x * c + partner * s


def _attn_prep_kernel(qa_ref, ka_ref, va_ref, qi_ref, ki_ref, misc_ref,
                      cosa_ref, sina_ref, cosi_ref, sini_ref, kgain_ref,
                      q_out, k_out, v_out, qi_out, ki_out, wi_out):
    ca, sa = cosa_ref[...], sina_ref[...]
    ci, si = cosi_ref[...], sini_ref[...]
    for h in range(ATTN_HEADS):
        sl = slice(h * LANES, (h + 1) * LANES)
        q_out[:, sl] = (_rope(qa_ref[:, sl], ca, sa, ATTN_ROT_HALF) * QK_SCALE_LOG2).astype(q_out.dtype)
    ones = jnp.ones((va_ref.shape[0], LANES), v_out.dtype)
    for h in range(ATTN_KV_HEADS):
        sl = slice(h * LANES, (h + 1) * LANES)
        k_out[:, sl] = _rope(ka_ref[:, sl], ca, sa, ATTN_ROT_HALF).astype(k_out.dtype)
        v_out[:, 2 * h * LANES:(2 * h + 1) * LANES] = va_ref[:, sl].astype(v_out.dtype)
        v_out[:, (2 * h + 1) * LANES:(2 * h + 2) * LANES] = ones
    for h in range(IDX_HEADS):
        sl = slice(h * LANES, (h + 1) * LANES)
        qi_out[:, sl] = _rope(qi_ref[:, sl], ci, si, IDX_ROT_HALF).astype(qi_out.dtype)
    ki = ki_ref[...]
    ms = jnp.sum(ki * ki, axis=-1, keepdims=True) * (1.0 / IDX_DIM)
    ki = ki * lax.rsqrt(ms + NORM_EPS) * kgain_ref[...]
    ki_out[...] = _rope(ki, ci, si, IDX_ROT_HALF).astype(ki_out.dtype)
    wi_out[...] = misc_ref[...] * (IDX_HEADS ** -0.5 * IDX_DIM ** -0.5)


def _attn_prep(p, tabs, kgain, tm):
    m = p.shape[0]
    cosa, sina, cosi, sini = tabs

    def col(width, off):
        blk = off // width
        return pl.BlockSpec((tm, width), lambda i: (i, blk))

    row = pl.BlockSpec((tm, LANES), lambda i: (i, 0))
    outs = [(ATTN_HEADS * LANES, MXU_DTYPE), (ATTN_KV_HEADS * LANES, MXU_DTYPE),
            (2 * ATTN_KV_HEADS * LANES, MXU_DTYPE), (IDX_HEADS * LANES, MXU_DTYPE),
            (LANES, MXU_DTYPE), (LANES, F32)]
    return pl.pallas_call(
        _attn_prep_kernel,
        grid=(m // tm,),
        in_specs=[col(1024, COL_QA), col(256, COL_KA), col(256, COL_VA), col(1024, COL_QI),
                  col(128, COL_KI), col(128, COL_MISC), row, row, row, row,
                  pl.BlockSpec((1, LANES), lambda i: (0, 0))],
        out_specs=[pl.BlockSpec((tm, w), lambda i: (i, 0)) for w, _ in outs],
        out_shape=[jax.ShapeDtypeStruct((m, w), d) for w, d in outs],
        compiler_params=_params(("parallel",)),
        name="attn_prep",
    )(p, p, p, p, p, p, cosa, sina, cosi, sini, kgain)


def _nt_dot(a, b):
    return lax.dot_general(a, b, (((1,), (1,)), ((), ())), preferred_element_type=F32)


def _dsa_kernel(q_ref, qi_ref, wi_ref, k_ref, v_ref, kidx_ref, u_ref, o_ref,
                keys_ref, m_ref, acc_ref, *, tq, tk, topk):
    i = pl.program_id(1)
    nkt = (i * tq + tq + tk - 1) // tk
    row_t = i * tq + lax.broadcasted_iota(I32, (tq, 1), 0)

    qi = qi_ref[...]
    wi = wi_ref[...]

    wb = [jnp.tile(jnp.broadcast_to(wi[:, MISC_WI + h:MISC_WI + h + 1], (tq, LANES)),
                   (1, tk // LANES)) for h in range(IDX_HEADS)]

    def index_tiles(kts):
        offs = [pl.multiple_of(kt * tk, tk) for kt in kts]
        raws = [[_nt_dot(qi[:, h * LANES:(h + 1) * LANES], kidx_ref[pl.ds(off, tk), :])
                 for h in range(IDX_HEADS)] for off in offs]
        for kt, off, raw in zip(kts, offs, raws):
            acc = jnp.maximum(raw[0], 0.0) * wb[0]
            for h in range(1, IDX_HEADS):
                acc = acc + jnp.maximum(raw[h], 0.0) * wb[h]
            bits = pltpu.bitcast(acc, I32)
            key = jnp.where(bits < 0, (bits ^ jnp.int32(0x7FFFFFFF)) + 1, bits)
            col = off + lax.broadcasted_iota(I32, (1, tk), 1)
            keys_ref[kt] = jnp.where(col <= row_t, key, jnp.int32(KEY_MIN))

    def idx_pair(j, carry):
        index_tiles([2 * j, 2 * j + 1])
        return carry

    lax.fori_loop(0, nkt // 2, idx_pair, 0)

    @pl.when(nkt % 2 == 1)
    def _():
        index_tiles([nkt - 1])

    def count_ge(cand):
        parts = []
        for r0 in range(0, tq, COUNT_ROWS):
            candb = jnp.broadcast_to(cand[r0:r0 + COUNT_ROWS], (COUNT_ROWS, LANES))

            def body(kt, acc, r0=r0, candb=candb):
                for c in range(tk // LANES):
                    kk = keys_ref[kt, r0:r0 + COUNT_ROWS, c * LANES:(c + 1) * LANES]
                    acc = acc + jnp.where(kk >= candb, 1.0, 0.0)
                return acc

            acc = lax.fori_loop(0, nkt, body, jnp.zeros((COUNT_ROWS, LANES), F32))
            parts.append(jnp.sum(acc, axis=1, keepdims=True))
        return jnp.concatenate(parts, axis=0)

    kf = float(topk)

    def bits_cond(c):
        b, _, cnt = c
        return (b < 32) & (jnp.max(jnp.abs(cnt - kf)) > 0.0)

    def bits_body(c):
        b, ans, cnt = c
        for u in range(BITS_PER_CHECK):
            cand = ans + lax.shift_left(jnp.int32(1), 31 - (b + u))
            cc = count_ge(cand)
            take = cc >= kf
            ans, cnt = jnp.where(take, cand, ans), jnp.where(take, cc, cnt)
        return b + BITS_PER_CHECK, ans, cnt

    _, thr, cnt = lax.while_loop(
        bits_cond, bits_body,
        (jnp.int32(0), jnp.full((tq, 1), KEY_MIN, I32), jnp.full((tq, 1), nkt * tk, I32).astype(F32)))
    short = thr == jnp.int32(KEY_MIN)
    has_tie = jnp.max(jnp.where(short, 0.0, cnt - kf)) > 0.0

    q = q_ref[...]
    qg = [jnp.concatenate([q[:, (g * ATTN_GROUP + hh) * LANES:(g * ATTN_GROUP + hh + 1) * LANES]
                           for hh in range(ATTN_GROUP)], axis=0) for g in range(ATTN_KV_HEADS)]
    m_ref[...] = jnp.full(m_ref.shape, NEG_BIG, F32)
    acc_ref[...] = jnp.zeros(acc_ref.shape, F32)
    lane_tiles = tk // LANES

    def attend(tiles):
        offs = [pl.multiple_of(kt * tk, tk) for kt, _ in tiles]
        s_all = [[_nt_dot(qg[g], k_ref[pl.ds(off, tk), g * LANES:(g + 1) * LANES])
                  for g in range(ATTN_KV_HEADS)] for off in offs]
        for (_, bias), off, s_kv in zip(tiles, offs, s_all):
            for g in range(ATTN_KV_HEADS):
                ps, alphas = [], []
                for hh in range(ATTN_GROUP):
                    h = g * ATTN_GROUP + hh
                    s = s_kv[g][hh * tq:(hh + 1) * tq] + bias
                    m_prev = m_ref[h]
                    m_new = jnp.maximum(m_prev, jnp.max(s, axis=1, keepdims=True))
                    m_ref[h] = m_new
                    alphas.append(jnp.exp2(m_prev - m_new))
                    ps.append(jnp.exp2(s - jnp.tile(m_new, (1, lane_tiles))).astype(MXU_DTYPE))
                v_t = v_ref[pl.ds(off, tk), 2 * g * LANES:(2 * g + 2) * LANES]
                pv = jnp.dot(jnp.concatenate(ps, axis=0), v_t, preferred_element_type=F32)
                for hh in range(ATTN_GROUP):
                    h = g * ATTN_GROUP + hh
                    acc_ref[h] = (jnp.tile(alphas[hh], (1, 2)) * acc_ref[h]
                                  + pv[hh * tq:(hh + 1) * tq])

    thr_floor = jnp.maximum(thr, jnp.int32(KEY_MIN + 1))

    def plain_bias(kt):
        return jnp.where(keys_ref[kt] >= thr_floor, 0.0, NEG_BIG)

    def pair_body(j, carry):
        attend([(2 * j, plain_bias(2 * j)), (2 * j + 1, plain_bias(2 * j + 1))])
        return carry

    def run_plain():
        lax.fori_loop(0, nkt // 2, pair_body, 0)

        @pl.when(nkt % 2 == 1)
        def _():
            attend([(nkt - 1, plain_bias(nkt - 1))])

    def run_ties():
        need = jnp.where(short, 0.0, kf - count_ge(thr + 1))

        def tie_body(kt, tie_carry):
            kk = keys_ref[kt]
            eq = kk == thr
            prefix = jnp.dot(jnp.where(eq, 1.0, 0.0).astype(BF16), u_ref[...],
                             preferred_element_type=F32)
            rank_ok = (prefix + tie_carry) <= need
            attend([(kt, jnp.where(kk > thr, 0.0,
                                   jnp.where(eq, jnp.where(rank_ok, 0.0, NEG_BIG), NEG_BIG)))])
            return tie_carry + prefix[:, tk - 1:tk]

        lax.fori_loop(0, nkt, tie_body, jnp.zeros((tq, 1), F32))

    lax.cond(has_tie, run_ties, run_plain)

    for h in range(ATTN_HEADS):
        acc = acc_ref[h]
        o_ref[:, h * LANES:(h + 1) * LANES] = (acc[:, :LANES] / acc[:, LANES:]).astype(o_ref.dtype)


def _dsa(q, k, v, qi, ki, wi, tie_u, tq, tk):
    b, s, _ = q.shape
    topk = min(INDEX_TOPK, s // 4)
    kern = functools.partial(_dsa_kernel, tq=tq, tk=tk, topk=topk)

    def qblk(w):
        return pl.BlockSpec((None, tq, w), lambda bi, i: (bi, i, 0))

    def full(w):
        return pl.BlockSpec((None, s, w), lambda bi, i: (bi, 0, 0))

    return pl.pallas_call(
        kern,
        grid=(b, s // tq),
        in_specs=[qblk(ATTN_HEADS * LANES), qblk(IDX_HEADS * LANES), qblk(LANES),
                  full(ATTN_KV_HEADS * LANES), full(2 * ATTN_KV_HEADS * LANES), full(LANES),
                  pl.BlockSpec((tk, tk), lambda bi, i: (0, 0))],
        out_specs=qblk(ATTN_HEADS * LANES),
        out_shape=jax.ShapeDtypeStruct((b, s, ATTN_HEADS * LANES), F32),
        scratch_shapes=[pltpu.VMEM((s // tk, tq, tk), I32),
                        pltpu.VMEM((ATTN_HEADS, tq, LANES), F32),
                        pltpu.VMEM((ATTN_HEADS, tq, 2 * LANES), F32)],
        compiler_params=_params(("parallel", "arbitrary")),
        name="dsa_attention",
    )(q, qi, wi, k, v, ki, tie_u)


def _shift_rows(x, halo, d):
    rolled = pltpu.roll(x, d, 0)
    head_rows = lax.broadcasted_iota(I32, (SUBLANES, x.shape[1]), 0)
    first = jnp.where(head_rows < d, pltpu.roll(halo, d, 0), rolled[:SUBLANES])
    return jnp.concatenate([first, rolled[SUBLANES:]], axis=0)


def _gdn_prep_kernel(q_ref, k_ref, v_ref, qh_ref, kh_ref, vh_ref, misc_ref,
                     wq_ref, wk_ref, wv_ref, alog_ref, dtb_ref,
                     qn_out, kn_out, vv_out, gcol_out, *, tm, seq):
    seq_start = (pl.program_id(0) * tm) % seq == 0

    def conv_silu(x_ref, halo_ref, w_ref):
        x = x_ref[...]
        halo = jnp.where(seq_start, 0.0, halo_ref[...])
        w = w_ref[...]
        y = x * w[CONV_K - 1:CONV_K]
        for d in range(1, CONV_K):
            y = y + _shift_rows(x, halo, d) * w[CONV_K - 1 - d:CONV_K - d]
        return y * jax.nn.sigmoid(y)

    def l2norm(x, out_ref, mult):
        for h in range(GDN_HEADS):
            sl = slice(h * LANES, (h + 1) * LANES)
            xh = x[:, sl]
            ss = jnp.sum(xh * xh, axis=-1, keepdims=True)
            out_ref[:, sl] = xh * (lax.rsqrt(ss + NORM_EPS) * mult)

    l2norm(conv_silu(q_ref, qh_ref, wq_ref), qn_out, GDN_DK ** -0.5)
    l2norm(conv_silu(k_ref, kh_ref, wk_ref), kn_out, 1.0)
    vv_out[...] = conv_silu(v_ref, vh_ref, wv_ref)

    misc = misc_ref[...]
    lane = lax.broadcasted_iota(I32, misc.shape, 1)
    row = lax.broadcasted_iota(I32, misc.shape, 0)
    is_a = (lane >= MISC_A) & (lane < MISC_A + GDN_HEADS)
    z = misc + dtb_ref[...]
    softplus = jnp.maximum(z, 0.0) + jnp.log(1.0 + jnp.exp(-jnp.abs(z)))
    g = jnp.where(is_a, -jnp.exp(alog_ref[...]) * softplus, 0.0)
    sh = 1
    while sh < CHUNK:
        g = g + jnp.where((row % CHUNK) >= sh, pltpu.roll(g, sh, 0), 0.0)
        sh *= 2
    is_beta = (lane >= MISC_BETA) & (lane < MISC_BETA + GDN_HEADS)
    gcol_out[...] = jnp.where(is_beta, jax.nn.sigmoid(misc), g)


def _gdn_prep(p, conv_w, alog_pad, dtb_pad, tm, seq):
    m = p.shape[0]
    wide = GDN_HEADS * LANES
    halo_blocks = tm // SUBLANES

    def col(off):
        blk = off // wide
        return pl.BlockSpec((tm, wide), lambda i: (i, blk))

    def halo(off):
        blk = off // wide
        return pl.BlockSpec((SUBLANES, wide), lambda i: (jnp.maximum(i * halo_blocks - 1, 0), blk))

    def wblk(j):
        return pl.BlockSpec((CONV_K, wide), lambda i: (0, j))

    one = pl.BlockSpec((1, LANES), lambda i: (0, 0))
    big = jax.ShapeDtypeStruct((m, wide), F32)
    kern = functools.partial(_gdn_prep_kernel, tm=tm, seq=seq)
    return pl.pallas_call(
        kern,
        grid=(m // tm,),
        in_specs=[col(COL_QB), col(COL_KB), col(COL_VB), halo(COL_QB), halo(COL_KB), halo(COL_VB),
                  pl.BlockSpec((tm, LANES), lambda i: (i, COL_MISC // LANES)),
                  wblk(0), wblk(1), wblk(2), one, one],
        out_specs=[pl.BlockSpec((tm, wide), lambda i: (i, 0))] * 3
        + [pl.BlockSpec((tm, LANES), lambda i: (i, 0))],
        out_shape=[big, big, big, jax.ShapeDtypeStruct((m, LANES), F32)],
        compiler_params=_params(("parallel",)),
        name="gdn_prep",
    )(p, p, p, p, p, p, p, conv_w, conv_w, conv_w, alog_pad, dtb_pad)


def _split(a):
    hi = a.astype(BF16)
    return hi, (a - hi.astype(F32)).astype(BF16)


def _dot3(a_split, b_split):
    (ah, al), (bh, bl) = a_split, b_split
    m = ah.shape[0]
    both = jnp.dot(jnp.concatenate([ah, al], axis=0), bh, preferred_element_type=F32)
    return both[:m] + both[m:] + jnp.dot(ah, bl, preferred_element_type=F32)


def _mxu_dot(a, b, dims=(((1,), (0,)), ((), ()))):
    return lax.dot_general(a.astype(MXU_DTYPE), b.astype(MXU_DTYPE), dims,
                           preferred_element_type=F32)


def _gdn_kernel(q_ref, k_ref, v_ref, gcol_ref, o_ref,
                state_ref, uw_st, qd_st, qk_st, kd_st, gt_st):
    h = pl.program_id(0)

    @pl.when(pl.program_id(1) == 0)
    def _():
        for ref in (state_ref, uw_st, qd_st, qk_st, kd_st, gt_st):
            ref[...] = jnp.zeros(ref.shape, F32)

    nb, rows = q_ref.shape[0], q_ref.shape[1]
    ii = lax.broadcasted_iota(I32, (CHUNK, CHUNK), 0)
    jj = lax.broadcasted_iota(I32, (CHUNK, CHUNK), 1)
    incl = ii >= jj
    strict = ii > jj
    eye = jnp.where(ii == jj, 1.0, 0.0)
    nt = (((1,), (1,)), ((), ()))
    tn = (((0,), (0,)), ((), ()))
    nsub = rows // CHUNK
    chains = [(b, sub) for b in range(nb) for sub in range(nsub)]

    states = [state_ref[b] for b in range(nb)]

    rec = {}

    def rec_read_state(sub):
        idx = [chains.index((b, sub)) for b in range(nb)]
        uws = [uw_st[c] for c in idx]
        rec["wss"] = [_mxu_dot(jnp.concatenate([uw[:, LANES:], qd_st[c]], axis=0), states[b])
                      for b, (c, uw) in enumerate(zip(idx, uws))]
        rec["v_news"] = [uw[:, :LANES] - ws[:CHUNK] for uw, ws in zip(uws, rec["wss"])]

    def rec_update(sub):
        sl = slice(sub * CHUNK, (sub + 1) * CHUNK)
        idx = [chains.index((b, sub)) for b in range(nb)]
        for b, c in enumerate(idx):
            o_ref[b, sl, :] = rec["wss"][b][CHUNK:] + _mxu_dot(qk_st[c], rec["v_news"][b])
        for b, c in enumerate(idx):
            states[b] = states[b] * gt_st[c][:1] + _mxu_dot(kd_st[c], rec["v_news"][b], tn)

    wide = GDN_PACK * CHUNK
    same_block = (lax.broadcasted_iota(I32, (wide, wide), 0) // CHUNK
                  == lax.broadcasted_iota(I32, (wide, wide), 1) // CHUNK)
    groups = [list(range(g, g + GDN_PACK)) for g in range(0, len(chains), GDN_PACK)]
    loc = {}

    def blockdiag(m):
        return jnp.where(same_block, jnp.tile(m, (GDN_PACK, 1)), jnp.zeros((), m.dtype))

    def packed_dot3(a_split, b_diag):
        (ah, al), (bh, bl) = a_split, b_diag
        both = jnp.dot(jnp.concatenate([ah, al], axis=0), bh, preferred_element_type=F32)
        return both[:CHUNK] + both[CHUNK:] + jnp.dot(ah, bl, preferred_element_type=F32)

    def local_start():
        col = {}
        for b in range(nb):
            gcol = gcol_ref[b]
            lane = lax.broadcasted_iota(I32, gcol.shape, 1)
            beta_all = jnp.sum(jnp.where(lane == MISC_BETA + h, gcol, 0.0), axis=1, keepdims=True)
            gc_all = jnp.sum(jnp.where(lane == MISC_A + h, gcol, 0.0), axis=1, keepdims=True)
            gc_sq = jnp.broadcast_to(gc_all, (rows, rows))
            col[b] = (beta_all, gc_all, gc_sq, gc_sq.T)
        xs, tinvs, rest = [], [], []
        for b, sub in chains:
            sl = slice(sub * CHUNK, (sub + 1) * CHUNK)
            beta_all, gc_all, gc_sq, gcr_sq = col[b]
            q, k, v = q_ref[b, sl, :], k_ref[b, sl, :], v_ref[b, sl, :]
            beta, gc = beta_all[sl], gc_all[sl]
            decay = jnp.where(incl, jnp.exp(gc_sq[sl, sl] - gcr_sq[sl, sl]), 0.0)
            kb = k * beta
            prods = _mxu_dot(jnp.concatenate([kb, q], axis=0), k, nt)
            x = -jnp.where(strict, prods[:CHUNK] * decay, 0.0)
            egc = jnp.exp(gc)
            g_last = gc[CHUNK - 1:CHUNK]
            xs.append(x)
            tinvs.append(eye + x)
            rest.append((jnp.concatenate([v * beta, kb * egc], axis=1), q * egc,
                         prods[CHUNK:] * decay, k * jnp.exp(g_last - gc), jnp.exp(g_last)))
        loc["rest"] = rest
        loc["xps"] = [_split(jnp.concatenate([xs[c] for c in grp], axis=1)) for grp in groups]
        loc["tps"] = [jnp.concatenate([tinvs[c] for c in grp], axis=1) for grp in groups]
        loc["diags"] = [(blockdiag(s[0]), blockdiag(s[1])) for s in loc["xps"]]

    def local_square():
        loc["xps"] = [_split(packed_dot3(s, d)) for s, d in zip(loc["xps"], loc["diags"])]
        loc["diags"] = [(blockdiag(s[0]), blockdiag(s[1])) for s in loc["xps"]]

    def local_accumulate():
        loc["tps"] = [t + packed_dot3(_split(t), d) for t, d in zip(loc["tps"], loc["diags"])]

    loc["uws"] = {}

    def local_solve(g):
        for j in range(GDN_PACK):
            c = groups[g][j]
            tinv = loc["tps"][g][:, j * CHUNK:(j + 1) * CHUNK]
            loc["uws"][c] = _dot3(_split(tinv), _split(loc["rest"][c][0]))

    def local_store():
        for c, r in enumerate(loc["rest"]):
            uw_st[c] = loc["uws"][c]
            qd_st[c] = r[1]
            qk_st[c] = r[2]
            kd_st[c] = r[3]
            gt_st[c] = jnp.broadcast_to(r[4], gt_st.shape[1:])

    levels = 0
    span = 1
    while 2 * span < CHUNK:
        levels += 1
        span *= 2
    local_stages = ([local_start] + [local_square, local_accumulate] * levels
                    + [functools.partial(local_solve, g) for g in range(len(groups))])
    rec_stages = [functools.partial(f, sub) for sub in range(nsub)
                  for f in (rec_read_state, rec_update)]
    for i in range(max(len(local_stages), len(rec_stages))):
        if i < len(local_stages):
            local_stages[i]()
        if i < len(rec_stages):
            rec_stages[i]()
    local_store()
    for b in range(nb):
        state_ref[b] = states[b]


def _gdn(qn, kn, vv, gcol, rows):
    batch, seq, _ = qn.shape
    nsteps = seq // rows
    nchains = batch * (rows // CHUNK)

    def cur(h, c):
        return (0, jnp.minimum(c, nsteps - 1), h)

    blk = pl.BlockSpec((batch, rows, LANES), cur)
    return pl.pallas_call(
        _gdn_kernel,
        grid=(GDN_HEADS, nsteps + 1),
        in_specs=[blk, blk, blk,
                  pl.BlockSpec((batch, rows, LANES), lambda h, c: (0, jnp.minimum(c, nsteps - 1), 0))],
        out_specs=pl.BlockSpec((batch, rows, LANES), lambda h, c: (0, jnp.maximum(c - 1, 0), h)),
        out_shape=jax.ShapeDtypeStruct(qn.shape, F32),
        scratch_shapes=[pltpu.VMEM((batch, GDN_DK, LANES), F32),
                        pltpu.VMEM((nchains, CHUNK, 2 * LANES), F32),
                        pltpu.VMEM((nchains, CHUNK, LANES), F32),
                        pltpu.VMEM((nchains, CHUNK, CHUNK), F32),
                        pltpu.VMEM((nchains, CHUNK, LANES), F32),
                        pltpu.VMEM((nchains, SUBLANES, LANES), F32)],
        compiler_params=_params(("parallel", "arbitrary")),
        name="gdn_chunks",
    )(qn, kn, vv, gcol)


def _merge_kernel(oa_ref, ob_ref, za_ref, zb_ref, ga_ref, gb_ref, x_ref, gng_ref, w_ref, fg_ref,
                  o_ref, mix_ref, *, final_norm):
    za = za_ref[...]
    ya = oa_ref[...] * (za * jax.nn.sigmoid(za))
    mix_ref[...] = (jax.nn.sigmoid(ga_ref[...]) * ya).astype(mix_ref.dtype)
    gng = gng_ref[...]
    for h in range(GDN_HEADS):
        sl = slice(h * LANES, (h + 1) * LANES)
        ob = ob_ref[:, sl]
        zb = zb_ref[:, sl]
        ms = jnp.mean(ob * ob, axis=-1, keepdims=True)
        yb = ob * lax.rsqrt(ms + NORM_EPS) * gng * (zb * jax.nn.sigmoid(zb))
        mix_ref[:, sl] += jax.nn.sigmoid(gb_ref[:, sl]) * yb
    y = x_ref[...] + jnp.dot(mix_ref[...].astype(MXU_DTYPE), w_ref[...], preferred_element_type=F32)
    if final_norm:
        ms = jnp.mean(y * y, axis=-1, keepdims=True)
        y = y * lax.rsqrt(ms + NORM_EPS) * fg_ref[...]
    o_ref[...] = y


def _merge_out(oa, ob, p, x2d, gdn_gain, w_out, final_gain, tm, final_norm):
    m = x2d.shape[0]

    def col(off):
        blk = off // D_MODEL
        return pl.BlockSpec((tm, D_MODEL), lambda i: (i, blk))

    row = pl.BlockSpec((tm, D_MODEL), lambda i: (i, 0))
    kern = functools.partial(_merge_kernel, final_norm=final_norm)
    return pl.pallas_call(
        kern,
        grid=(m // tm,),
        in_specs=[row, row, col(COL_ZA), col(COL_ZB), col(COL_GA), col(COL_GB), row,
                  pl.BlockSpec((1, LANES), lambda i: (0, 0)),
                  pl.BlockSpec((D_MODEL, D_MODEL), lambda i: (0, 0)),
                  pl.BlockSpec((1, D_MODEL), lambda i: (0, 0))],
        out_specs=row,
        out_shape=jax.ShapeDtypeStruct((m, D_MODEL), F32),
        scratch_shapes=[pltpu.VMEM((tm, D_MODEL), F32)],
        compiler_params=_params(("parallel",)),
        name="merge_out_proj",
    )(oa, ob, p, p, p, p, x2d, gdn_gain, w_out, final_gain)


def _pack_w_in(w):
    edges = [0]
    for n in IN_SPLITS:
        edges.append(edges[-1] + n)
    (qa, ka, va, za, qi, ki, wi, qkvb, zb, beta, a, gates) = [
        w[:, edges[j]:edges[j + 1]] for j in range(len(IN_SPLITS))]
    d = w.shape[0]
    qi = jnp.pad(qi.reshape(d, IDX_HEADS, IDX_DIM), ((0, 0), (0, 0), (0, LANES - IDX_DIM)))
    qi = qi.reshape(d, IDX_HEADS * LANES)
    ki = jnp.pad(ki, ((0, 0), (0, LANES - IDX_DIM)))
    misc = jnp.pad(jnp.concatenate([wi, beta, a], axis=1), ((0, 0), (0, LANES - 3 * 8)))
    packed = jnp.concatenate(
        [qa, za, zb, gates[:, :D_MODEL], gates[:, D_MODEL:], qkvb, qi, ka, va, ki, misc], axis=1)
    return packed.astype(MXU_DTYPE)


def _lane_pad(v, off):
    return jnp.pad(v.astype(F32), (off, LANES - off - v.shape[0])).reshape(1, LANES)


def _pick(n, prefs):
    for t in prefs:
        if n % t == 0:
            return t
    raise ValueError(f"no tile for {n}")


def kernel(x, positions, norm_gain, w_in, conv_w, a_log, dt_bias, gdn_norm_gain, idx_k_gain, w_out,
           final_gain):
    batch, seq, d = x.shape
    assert d == D_MODEL and seq % 128 == 0
    depth = w_in.shape[0]
    m = batch * seq
    tm_proj = _pick(m, (1024, 512, 256, 128))
    tn_proj = 768
    tm_prep = _pick(seq, (512, 256, 128))
    tm_merge = _pick(m, (256, 128))
    tq = 128
    tk = _pick(seq, (512, 256, 128))

    lane = jnp.arange(LANES)

    def inv_freq(half):
        rot = 2 * half
        f = ROPE_THETA ** (-jnp.arange(0, rot, 2, dtype=F32) / rot)
        return jnp.where(lane < rot, f[lane % half], 0.0).reshape(1, LANES).astype(F32)

    tabs = _rope_tables(positions.reshape(m, 1).astype(I32), inv_freq(ATTN_ROT_HALF),
                        inv_freq(IDX_ROT_HALF), tm_prep)
    tie_u = (jnp.arange(tk)[:, None] <= jnp.arange(tk)[None, :]).astype(BF16)

    x2d = x.reshape(m, d)
    for layer in range(depth):
        p = _project(x2d, norm_gain[layer].reshape(1, d), _pack_w_in(w_in[layer]), tm_proj, tn_proj)

        kgain = _lane_pad(idx_k_gain[layer], 0)
        q, k, v, qi, ki, wi = _attn_prep(p, tabs, kgain, tm_prep)
        shp = lambda t: t.reshape(batch, seq, t.shape[-1])
        o_a = _dsa(shp(q), shp(k), shp(v), shp(qi), shp(ki), shp(wi), tie_u, tq, tk).reshape(m, d)

        qn, kn, vv, gcol = _gdn_prep(p, conv_w[layer], _lane_pad(a_log[layer], MISC_A),
                                     _lane_pad(dt_bias[layer], MISC_A), tm_prep, seq)
        o_b = _gdn(shp(qn), shp(kn), shp(vv), shp(gcol), 8 * CHUNK).reshape(m, d)

        x2d = _merge_out(o_a, o_b, p, x2d, gdn_norm_gain[layer].reshape(1, LANES),
                         w_out[layer].astype(MXU_DTYPE), final_gain.reshape(1, d), tm_merge,
                         final_norm=(layer == depth - 1))
    return x2d.reshape(batch, seq, d)
```

```python
import functools
import math

import jax
import jax.numpy as jnp
from jax import lax
from jax.experimental import pallas as pl
from jax.experimental.pallas import tpu as pltpu

F32 = jnp.float32
BF16 = jnp.bfloat16
I32 = jnp.int32

D_MODEL = 1024
ATTN_HEADS = 8
ATTN_KV_HEADS = 2
HEAD_DIM = 128
ATTN_GROUP = ATTN_HEADS // ATTN_KV_HEADS
ATTN_ROT_HALF = 16
IDX_HEADS = 8
IDX_DIM = 64
IDX_ROT_HALF = 8
INDEX_TOPK = 256
ROPE_THETA = 500000.0
GDN_HEADS = 8
GDN_DK = 128
CONV_K = 4
CHUNK = 64
NORM_EPS = 1e-6
IN_SPLITS = (1024, 256, 256, 1024, 512, 64, 8, 3072, 1024, 8, 8, 2048)

LANES = 128
SUBLANES = 8
VMEM_LIMIT = 56 * 1024 * 1024

COL_QA, COL_ZA, COL_ZB, COL_GA, COL_GB = 0, 1024, 2048, 3072, 4096
COL_QB, COL_KB, COL_VB, COL_QI = 5120, 6144, 7168, 8192
COL_KA, COL_VA, COL_KI, COL_MISC = 9216, 9472, 9728, 9856
P_COLS = 9984
MISC_WI, MISC_BETA, MISC_A = 0, 8, 16

MXU_DTYPE = BF16
GDN_PRECISION = lax.Precision.HIGHEST

KEY_MIN = -2 ** 31
NEG_BIG = -1e30
QK_SCALE_LOG2 = HEAD_DIM ** -0.5 * math.log2(math.e)
COUNT_ROWS = 128
BITS_PER_CHECK = 4
GDN_PACK = 2


def _params(sem):
    return pltpu.CompilerParams(dimension_semantics=sem, vmem_limit_bytes=VMEM_LIMIT)


def _proj_kernel(x_ref, gain_ref, w_ref, o_ref, h_ref):
    @pl.when(pl.program_id(1) == 0)
    def _():
        x = x_ref[...]
        ms = jnp.mean(x * x, axis=-1, keepdims=True)
        h_ref[...] = (x * lax.rsqrt(ms + NORM_EPS) * gain_ref[...]).astype(h_ref.dtype)

    o_ref[...] = jnp.dot(h_ref[...], w_ref[...], preferred_element_type=F32)


def _project(x2d, gain, w_packed, tm, tn):
    m = x2d.shape[0]
    return pl.pallas_call(
        _proj_kernel,
        grid=(m // tm, P_COLS // tn),
        in_specs=[
            pl.BlockSpec((tm, D_MODEL), lambda i, j: (i, 0)),
            pl.BlockSpec((1, D_MODEL), lambda i, j: (0, 0)),
            pl.BlockSpec((D_MODEL, tn), lambda i, j: (0, j)),
        ],
        out_specs=pl.BlockSpec((tm, tn), lambda i, j: (i, j)),
        out_shape=jax.ShapeDtypeStruct((m, P_COLS), F32),
        scratch_shapes=[pltpu.VMEM((tm, D_MODEL), MXU_DTYPE)],
        compiler_params=_params(("parallel", "arbitrary")),
        name="in_proj",
    )(x2d, gain, w_packed)


def _rope_tab_kernel(pos_ref, inva_ref, invi_ref, cosa_ref, sina_ref, cosi_ref, sini_ref):
    pos = pos_ref[...].astype(F32)
    lane = lax.broadcasted_iota(I32, (1, LANES), 1)

    def tables(inv, half, cos_ref, sin_ref):
        ang = pos * inv
        c = jnp.cos(ang)
        s = jnp.sin(ang)
        cos_ref[...] = jnp.where(lane < 2 * half, c, 1.0)
        sin_ref[...] = jnp.where(lane < half, -s, jnp.where(lane < 2 * half, s, 0.0))

    tables(inva_ref[...], ATTN_ROT_HALF, cosa_ref, sina_ref)
    tables(invi_ref[...], IDX_ROT_HALF, cosi_ref, sini_ref)


def _rope_tables(pos2d, inv_a, inv_i, tm):
    m = pos2d.shape[0]
    tab = jax.ShapeDtypeStruct((m, LANES), F32)
    row = pl.BlockSpec((tm, LANES), lambda i: (i, 0))
    one = pl.BlockSpec((1, LANES), lambda i: (0, 0))
    return pl.pallas_call(
        _rope_tab_kernel,
        grid=(m // tm,),
        in_specs=[pl.BlockSpec((tm, 1), lambda i: (i, 0)), one, one],
        out_specs=[row, row, row, row],
        out_shape=[tab, tab, tab, tab],
        compiler_params=_params(("parallel",)),
        name="rope_tables",
    )(pos2d, inv_a, inv_i)


def _rope(x, c, s, half):
    lane = lax.broadcasted_iota(I32, x.shape, 1)
    partner = jnp.where(lane < half, pltpu.roll(x, LANES - half, 1), pltpu.roll(x, half, 1))
    return x * c + partner * s


def _attn_prep_kernel(qa_ref, ka_ref, va_ref, qi_ref, ki_ref, misc_ref,
                      cosa_ref, sina_ref, cosi_ref, sini_ref, kgain_ref,
                      q_out, k_out, v_out, qi_out, ki_out, wi_out):
    ca, sa = cosa_ref[...], sina_ref[...]
    ci, si = cosi_ref[...], sini_ref[...]
    for h in range(ATTN_HEADS):
        sl = slice(h * LANES, (h + 1) * LANES)
        q_out[:, sl] = (_rope(qa_ref[:, sl], ca, sa, ATTN_ROT_HALF) * QK_SCALE_LOG2).astype(q_out.dtype)
    ones = jnp.ones((va_ref.shape[0], LANES), v_out.dtype)
    for h in range(ATTN_KV_HEADS):
        sl = slice(h * LANES, (h + 1) * LANES)
        k_out[:, sl] = _rope(ka_ref[:, sl], ca, sa, ATTN_ROT_HALF).astype(k_out.dtype)
        v_out[:, 2 * h * LANES:(2 * h + 1) * LANES] = va_ref[:, sl].astype(v_out.dtype)
        v_out[:, (2 * h + 1) * LANES:(2 * h + 2) * LANES] = ones
    for h in range(IDX_HEADS):
        sl = slice(h * LANES, (h + 1) * LANES)
        qi_out[:, sl] = _rope(qi_ref[:, sl], ci, si, IDX_ROT_HALF).astype(qi_out.dtype)
    ki = ki_ref[...]
    ms = jnp.sum(ki * ki, axis=-1, keepdims=True) * (1.0 / IDX_DIM)
    ki = ki * lax.rsqrt(ms + NORM_EPS) * kgain_ref[...]
    ki_out[...] = _rope(ki, ci, si, IDX_ROT_HALF).astype(ki_out.dtype)
    wi_out[...] = misc_ref[...] * (IDX_HEADS ** -0.5 * IDX_DIM ** -0.5)


def _attn_prep(p, tabs, kgain, tm):
    m = p.shape[0]
    cosa, sina, cosi, sini = tabs

    def col(width, off):
        blk = off // width
        return pl.BlockSpec((tm, width), lambda i: (i, blk))

    row = pl.BlockSpec((tm, LANES), lambda i: (i, 0))
    outs = [(ATTN_HEADS * LANES, MXU_DTYPE), (ATTN_KV_HEADS * LANES, MXU_DTYPE),
            (2 * ATTN_KV_HEADS * LANES, MXU_DTYPE), (IDX_HEADS * LANES, MXU_DTYPE),
            (LANES, MXU_DTYPE), (LANES, F32)]
    return pl.pallas_call(
        _attn_prep_kernel,
        grid=(m // tm,),
        in_specs=[col(1024, COL_QA), col(256, COL_KA), col(256, COL_VA), col(1024, COL_QI),
                  col(128, COL_KI), col(128, COL_MISC), row, row, row, row,
                  pl.BlockSpec((1, LANES), lambda i: (0, 0))],
        out_specs=[pl.BlockSpec((tm, w), lambda i: (i, 0)) for w, _ in outs],
        out_shape=[jax.ShapeDtypeStruct((m, w), d) for w, d in outs],
        compiler_params=_params(("parallel",)),
        name="attn_prep",
    )(p, p, p, p, p, p, cosa, sina, cosi, sini, kgain)


def _nt_dot(a, b):
    return lax.dot_general(a, b, (((1,), (1,)), ((), ())), preferred_element_type=F32)


def _dsa_kernel(q_ref, qi_ref, wi_ref, k_ref, v_ref, kidx_ref, u_ref, o_ref,
                keys_ref, keys_hi_ref, m_ref, acc_ref, *, tq, tk, topk):
    i = pl.program_id(1)
    nkt = (i * tq + tq + tk - 1) // tk
    row_t = i * tq + lax.broadcasted_iota(I32, (tq, 1), 0)

    qi = qi_ref[...]
    wi = wi_ref[...]

    wb = [jnp.tile(jnp.broadcast_to(wi[:, MISC_WI + h:MISC_WI + h + 1], (tq, LANES)),
                   (1, tk // LANES)) for h in range(IDX_HEADS)]

    def index_tiles(kts):
        offs = [pl.multiple_of(kt * tk, tk) for kt in kts]
        raws = [[_nt_dot(qi[:, h * LANES:(h + 1) * LANES], kidx_ref[pl.ds(off, tk), :])
                 for h in range(IDX_HEADS)] for off in offs]
        for kt, off, raw in zip(kts, offs, raws):
            acc = jnp.maximum(raw[0], 0.0) * wb[0]
            for h in range(1, IDX_HEADS):
                acc = acc + jnp.maximum(raw[h], 0.0) * wb[h]
            bits = pltpu.bitcast(acc, I32)
            key = jnp.where(bits < 0, (bits ^ jnp.int32(0x7FFFFFFF)) + 1, bits)
            col = off + lax.broadcasted_iota(I32, (1, tk), 1)
            key = jnp.where(col <= row_t, key, jnp.int32(KEY_MIN))
            keys_ref[kt] = key
            keys_hi_ref[kt] = lax.shift_right_arithmetic(key, 16).astype(keys_hi_ref.dtype)

    def idx_pair(j, carry):
        index_tiles([2 * j, 2 * j + 1])
        return carry

    lax.fori_loop(0, nkt // 2, idx_pair, 0)

    @pl.when(nkt % 2 == 1)
    def _():
        index_tiles([nkt - 1])

    def count_in(ref, cand):
        parts = []
        for r0 in range(0, tq, COUNT_ROWS):
            candb = jnp.broadcast_to(cand[r0:r0 + COUNT_ROWS], (COUNT_ROWS, LANES)).astype(ref.dtype)
            one_, zero_ = jnp.ones((), ref.dtype), jnp.zeros((), ref.dtype)

            def one(kt, acc, r0=r0, candb=candb, one_=one_, zero_=zero_):
                for c in range(tk // LANES):
                    kk = ref[kt, r0:r0 + COUNT_ROWS, c * LANES:(c + 1) * LANES]
                    acc = acc + jnp.where(kk >= candb, one_, zero_)
                return acc

            def two(j, acc, one=one):
                return one(2 * j + 1, one(2 * j, acc))

            acc = lax.fori_loop(0, nkt // 2, two, jnp.zeros((COUNT_ROWS, LANES), ref.dtype))
            acc = lax.cond(nkt % 2 == 1, functools.partial(one, nkt - 1), lambda a: a, acc)
            parts.append(jnp.sum(acc.astype(F32), axis=1, keepdims=True))
        return jnp.concatenate(parts, axis=0)

    def count_ge(cand):
        return count_in(keys_ref, cand)

    def count_ge_hi(cand):
        return count_in(keys_hi_ref, lax.shift_right_arithmetic(cand, 16))

    kf = float(topk)

    def search(counter, first_bit, last_bit, ans, cnt):
        def cond(c):
            b, _, cnt = c
            return (b < last_bit) & (jnp.max(jnp.abs(cnt - kf)) > 0.0)

        def body(c):
            b, ans, cnt = c
            for u in range(BITS_PER_CHECK):
                cand = ans + lax.shift_left(jnp.int32(1), 31 - (b + u))
                cc = counter(cand)
                take = cc >= kf
                ans, cnt = jnp.where(take, cand, ans), jnp.where(take, cc, cnt)
            return b + BITS_PER_CHECK, ans, cnt

        _, ans, cnt = lax.while_loop(cond, body, (jnp.int32(first_bit), ans, cnt))
        return ans, cnt

    thr, cnt = search(count_ge_hi, 0, 16, jnp.full((tq, 1), KEY_MIN, I32),
                      jnp.full((tq, 1), nkt * tk, I32).astype(F32))
    thr, cnt = search(count_ge, 16, 32, thr, cnt)
    short = thr == jnp.int32(KEY_MIN)
    has_tie = jnp.max(jnp.where(short, 0.0, cnt - kf)) > 0.0

    q = q_ref[...]
    qg = [jnp.concatenate([q[:, (g * ATTN_GROUP + hh) * LANES:(g * ATTN_GROUP + hh + 1) * LANES]
                           for hh in range(ATTN_GROUP)], axis=0) for g in range(ATTN_KV_HEADS)]
    m_ref[...] = jnp.full(m_ref.shape, NEG_BIG, F32)
    acc_ref[...] = jnp.zeros(acc_ref.shape, F32)
    lane_tiles = tk // LANES

    def attend(tiles):
        units = [(pl.multiple_of(kt * tk, tk), bias, g) for kt, bias in tiles
                 for g in range(ATTN_KV_HEADS)]

        def qk(unit):
            off, _, g = unit
            return _nt_dot(qg[g], k_ref[pl.ds(off, tk), g * LANES:(g + 1) * LANES])

        ahead = 2
        scores = [qk(u) for u in units[:ahead]]
        for i, (off, bias, g) in enumerate(units):
            ps, alphas = [], []
            for hh in range(ATTN_GROUP):
                h = g * ATTN_GROUP + hh
                s = scores[i][hh * tq:(hh + 1) * tq] + bias
                m_prev = m_ref[h]
                m_new = jnp.maximum(m_prev, jnp.max(s, axis=1, keepdims=True))
                m_ref[h] = m_new
                alphas.append(jnp.exp2(m_prev - m_new))
                ps.append(jnp.exp2(s - jnp.tile(m_new, (1, lane_tiles))).astype(MXU_DTYPE))
            if i + ahead < len(units):
                scores.append(qk(units[i + ahead]))
            v_t = v_ref[pl.ds(off, tk), 2 * g * LANES:(2 * g + 2) * LANES]
            pv = jnp.dot(jnp.concatenate(ps, axis=0), v_t, preferred_element_type=F32)
            for hh in range(ATTN_GROUP):
                h = g * ATTN_GROUP + hh
                acc_ref[h] = (jnp.tile(alphas[hh], (1, 2)) * acc_ref[h]
                              + pv[hh * tq:(hh + 1) * tq])

    thr_floor = jnp.maximum(thr, jnp.int32(KEY_MIN + 1))

    def plain_bias(kt):
        return jnp.where(keys_ref[kt] >= thr_floor, 0.0, NEG_BIG)

    def pair_body(j, carry):
        attend([(2 * j, plain_bias(2 * j)), (2 * j + 1, plain_bias(2 * j + 1))])
        return carry

    def run_plain():
        lax.fori_loop(0, nkt // 2, pair_body, 0)

        @pl.when(nkt % 2 == 1)
        def _():
            attend([(nkt - 1, plain_bias(nkt - 1))])

    def run_ties():
        need = jnp.where(short, 0.0, kf - count_ge(thr + 1))

        def tie_body(kt, tie_carry):
            kk = keys_ref[kt]
            eq = kk == thr
            prefix = jnp.dot(jnp.where(eq, 1.0, 0.0).astype(BF16), u_ref[...],
                             preferred_element_type=F32)
            rank_ok = (prefix + tie_carry) <= need
            attend([(kt, jnp.where(kk > thr, 0.0,
                                   jnp.where(eq, jnp.where(rank_ok, 0.0, NEG_BIG), NEG_BIG)))])
            return tie_carry + prefix[:, tk - 1:tk]

        lax.fori_loop(0, nkt, tie_body, jnp.zeros((tq, 1), F32))

    lax.cond(has_tie, run_ties, run_plain)

    for h in range(ATTN_HEADS):
        acc = acc_ref[h]
        o_ref[:, h * LANES:(h + 1) * LANES] = (acc[:, :LANES] / acc[:, LANES:]).astype(o_ref.dtype)


def _dsa(q, k, v, qi, ki, wi, tie_u, tq, tk):
    b, s, _ = q.shape
    topk = min(INDEX_TOPK, s // 4)
    kern = functools.partial(_dsa_kernel, tq=tq, tk=tk, topk=topk)

    def qblk(w):
        return pl.BlockSpec((None, tq, w), lambda bi, i: (bi, i, 0))

    def full(w):
        return pl.BlockSpec((None, s, w), lambda bi, i: (bi, 0, 0))

    return pl.pallas_call(
        kern,
        grid=(b, s // tq),
        in_specs=[qblk(ATTN_HEADS * LANES), qblk(IDX_HEADS * LANES), qblk(LANES),
                  full(ATTN_KV_HEADS * LANES), full(2 * ATTN_KV_HEADS * LANES), full(LANES),
                  pl.BlockSpec((tk, tk), lambda bi, i: (0, 0))],
        out_specs=qblk(ATTN_HEADS * LANES),
        out_shape=jax.ShapeDtypeStruct((b, s, ATTN_HEADS * LANES), F32),
        scratch_shapes=[pltpu.VMEM((s // tk, tq, tk), I32),
                        pltpu.VMEM((s // tk, tq, tk), jnp.int16),
                        pltpu.VMEM((ATTN_HEADS, tq, LANES), F32),
                        pltpu.VMEM((ATTN_HEADS, tq, 2 * LANES), F32)],
        compiler_params=_params(("parallel", "arbitrary")),
        name="dsa_attention",
    )(q, qi, wi, k, v, ki, tie_u)


def _shift_rows(x, halo, d):
    rolled = pltpu.roll(x, d, 0)
    head_rows = lax.broadcasted_iota(I32, (SUBLANES, x.shape[1]), 0)
    first = jnp.where(head_rows < d, pltpu.roll(halo, d, 0), rolled[:SUBLANES])
    return jnp.concatenate([first, rolled[SUBLANES:]], axis=0)


def _gdn_prep_kernel(q_ref, k_ref, v_ref, qh_ref, kh_ref, vh_ref, misc_ref,
                     wq_ref, wk_ref, wv_ref, alog_ref, dtb_ref,
                     qn_out, kn_out, vv_out, gcol_out, *, tm, seq):
    seq_start = (pl.program_id(0) * tm) % seq == 0

    def conv_silu(x_ref, halo_ref, w_ref):
        x = x_ref[...]
        halo = jnp.where(seq_start, 0.0, halo_ref[...])
        w = w_ref[...]
        y = x * w[CONV_K - 1:CONV_K]
        for d in range(1, CONV_K):
            y = y + _shift_rows(x, halo, d) * w[CONV_K - 1 - d:CONV_K - d]
        return y * jax.nn.sigmoid(y)

    def l2norm(x, out_ref, mult):
        for h in range(GDN_HEADS):
            sl = slice(h * LANES, (h + 1) * LANES)
            xh = x[:, sl]
            ss = jnp.sum(xh * xh, axis=-1, keepdims=True)
            out_ref[:, sl] = xh * (lax.rsqrt(ss + NORM_EPS) * mult)

    l2norm(conv_silu(q_ref, qh_ref, wq_ref), qn_out, GDN_DK ** -0.5)
    l2norm(conv_silu(k_ref, kh_ref, wk_ref), kn_out, 1.0)
    vv_out[...] = conv_silu(v_ref, vh_ref, wv_ref)

    misc = misc_ref[...]
    lane = lax.broadcasted_iota(I32, misc.shape, 1)
    row = lax.broadcasted_iota(I32, misc.shape, 0)
    is_a = (lane >= MISC_A) & (lane < MISC_A + GDN_HEADS)
    z = misc + dtb_ref[...]
    softplus = jnp.maximum(z, 0.0) + jnp.log(1.0 + jnp.exp(-jnp.abs(z)))
    g = jnp.where(is_a, -jnp.exp(alog_ref[...]) * softplus, 0.0)
    sh = 1
    while sh < CHUNK:
        g = g + jnp.where((row % CHUNK) >= sh, pltpu.roll(g, sh, 0), 0.0)
        sh *= 2
    is_beta = (lane >= MISC_BETA) & (lane < MISC_BETA + GDN_HEADS)
    gcol_out[...] = jnp.where(is_beta, jax.nn.sigmoid(misc), g)


def _gdn_prep(p, conv_w, alog_pad, dtb_pad, tm, seq):
    m = p.shape[0]
    wide = GDN_HEADS * LANES
    halo_blocks = tm // SUBLANES

    def col(off):
        blk = off // wide
        return pl.BlockSpec((tm, wide), lambda i: (i, blk))

    def halo(off):
        blk = off // wide
        return pl.BlockSpec((SUBLANES, wide), lambda i: (jnp.maximum(i * halo_blocks - 1, 0), blk))

    def wblk(j):
        return pl.BlockSpec((CONV_K, wide), lambda i: (0, j))

    one = pl.BlockSpec((1, LANES), lambda i: (0, 0))
    big = jax.ShapeDtypeStruct((m, wide), F32)
    kern = functools.partial(_gdn_prep_kernel, tm=tm, seq=seq)
    return pl.pallas_call(
        kern,
        grid=(m // tm,),
        in_specs=[col(COL_QB), col(COL_KB), col(COL_VB), halo(COL_QB), halo(COL_KB), halo(COL_VB),
                  pl.BlockSpec((tm, LANES), lambda i: (i, COL_MISC // LANES)),
                  wblk(0), wblk(1), wblk(2), one, one],
        out_specs=[pl.BlockSpec((tm, wide), lambda i: (i, 0))] * 3
        + [pl.BlockSpec((tm, LANES), lambda i: (i, 0))],
        out_shape=[big, big, big, jax.ShapeDtypeStruct((m, LANES), F32)],
        compiler_params=_params(("parallel",)),
        name="gdn_prep",
    )(p, p, p, p, p, p, p, conv_w, conv_w, conv_w, alog_pad, dtb_pad)


def _split(a):
    hi = a.astype(BF16)
    return hi, (a - hi.astype(F32)).astype(BF16)


def _dot3(a_split, b_split):
    (ah, al), (bh, bl) = a_split, b_split
    m = ah.shape[0]
    both = jnp.dot(jnp.concatenate([ah, al], axis=0), bh, preferred_element_type=F32)
    return both[:m] + both[m:] + jnp.dot(ah, bl, preferred_element_type=F32)


def _mxu_dot(a, b, dims=(((1,), (0,)), ((), ()))):
    return lax.dot_general(a.astype(MXU_DTYPE), b.astype(MXU_DTYPE), dims,
                           preferred_element_type=F32)


def _gdn_kernel(q_ref, k_ref, v_ref, gcol_ref, o_ref,
                state_ref, uw_st, qd_st, qk_st, kd_st, gt_st):
    h = pl.program_id(0)

    @pl.when(pl.program_id(1) == 0)
    def _():
        for ref in (state_ref, uw_st, qd_st, qk_st, kd_st, gt_st):
            ref[...] = jnp.zeros(ref.shape, F32)

    nb, rows = q_ref.shape[0], q_ref.shape[1]
    ii = lax.broadcasted_iota(I32, (CHUNK, CHUNK), 0)
    jj = lax.broadcasted_iota(I32, (CHUNK, CHUNK), 1)
    incl = ii >= jj
    strict = ii > jj
    eye = jnp.where(ii == jj, 1.0, 0.0)
    nt = (((1,), (1,)), ((), ()))
    tn = (((0,), (0,)), ((), ()))
    nsub = rows // CHUNK
    chains = [(b, sub) for b in range(nb) for sub in range(nsub)]

    states = [state_ref[b] for b in range(nb)]

    rec = {}

    def rec_read_state(sub):
        idx = [chains.index((b, sub)) for b in range(nb)]
        uws = [uw_st[c] for c in idx]
        rec["wss"] = [_mxu_dot(jnp.concatenate([uw[:, LANES:], qd_st[c]], axis=0), states[b])
                      for b, (c, uw) in enumerate(zip(idx, uws))]
        rec["v_news"] = [uw[:, :LANES] - ws[:CHUNK] for uw, ws in zip(uws, rec["wss"])]

    def rec_update(sub):
        sl = slice(sub * CHUNK, (sub + 1) * CHUNK)
        idx = [chains.index((b, sub)) for b in range(nb)]
        for b, c in enumerate(idx):
            o_ref[b, sl, :] = rec["wss"][b][CHUNK:] + _mxu_dot(qk_st[c], rec["v_news"][b])
        for b, c in enumerate(idx):
            states[b] = states[b] * gt_st[c][:1] + _mxu_dot(kd_st[c], rec["v_news"][b], tn)

    wide = GDN_PACK * CHUNK
    same_block = (lax.broadcasted_iota(I32, (wide, wide), 0) // CHUNK
                  == lax.broadcasted_iota(I32, (wide, wide), 1) // CHUNK)
    groups = [list(range(g, g + GDN_PACK)) for g in range(0, len(chains), GDN_PACK)]
    loc = {}

    def blockdiag(m):
        return jnp.where(same_block, jnp.tile(m, (GDN_PACK, 1)), jnp.zeros((), m.dtype))

    def packed_dot3(a_split, b_diag):
        (ah, al), (bh, bl) = a_split, b_diag
        both = jnp.dot(jnp.concatenate([ah, al], axis=0), bh, preferred_element_type=F32)
        return both[:CHUNK] + both[CHUNK:] + jnp.dot(ah, bl, preferred_element_type=F32)

    def local_start():
        col = {}
        for b in range(nb):
            gcol = gcol_ref[b]
            lane = lax.broadcasted_iota(I32, gcol.shape, 1)
            beta_all = jnp.sum(jnp.where(lane == MISC_BETA + h, gcol, 0.0), axis=1, keepdims=True)
            gc_all = jnp.sum(jnp.where(lane == MISC_A + h, gcol, 0.0), axis=1, keepdims=True)
            gc_sq = jnp.broadcast_to(gc_all, (rows, rows))
            col[b] = (beta_all, gc_all, gc_sq, gc_sq.T)
        xs, tinvs, rest = [], [], []
        for b, sub in chains:
            sl = slice(sub * CHUNK, (sub + 1) * CHUNK)
            beta_all, gc_all, gc_sq, gcr_sq = col[b]
            q, k, v = q_ref[b, sl, :], k_ref[b, sl, :], v_ref[b, sl, :]
            beta, gc = beta_all[sl], gc_all[sl]
            decay = jnp.where(incl, jnp.exp(gc_sq[sl, sl] - gcr_sq[sl, sl]), 0.0)
            kb = k * beta
            prods = _mxu_dot(jnp.concatenate([kb, q], axis=0), k, nt)
            x = -jnp.where(strict, prods[:CHUNK] * decay, 0.0)
            egc = jnp.exp(gc)
            g_last = gc[CHUNK - 1:CHUNK]
            xs.append(x)
            tinvs.append(eye + x)
            rest.append((jnp.concatenate([v * beta, kb * egc], axis=1), q * egc,
                         prods[CHUNK:] * decay, k * jnp.exp(g_last - gc), jnp.exp(g_last)))
        loc["rest"] = rest
        loc["xps"] = [_split(jnp.concatenate([xs[c] for c in grp], axis=1)) for grp in groups]
        loc["tps"] = [jnp.concatenate([tinvs[c] for c in grp], axis=1) for grp in groups]
        loc["diags"] = [(blockdiag(s[0]), blockdiag(s[1])) for s in loc["xps"]]

    def local_square():
        loc["xps"] = [_split(packed_dot3(s, d)) for s, d in zip(loc["xps"], loc["diags"])]
        loc["diags"] = [(blockdiag(s[0]), blockdiag(s[1])) for s in loc["xps"]]

    def local_accumulate():
        loc["tps"] = [t + packed_dot3(_split(t), d) for t, d in zip(loc["tps"], loc["diags"])]

    loc["uws"] = {}

    def local_solve(g):
        for j in range(GDN_PACK):
            c = groups[g][j]
            tinv = loc["tps"][g][:, j * CHUNK:(j + 1) * CHUNK]
            loc["uws"][c] = _dot3(_split(tinv), _split(loc["rest"][c][0]))

    def local_store():
        for c, r in enumerate(loc["rest"]):
            uw_st[c] = loc["uws"][c]
            qd_st[c] = r[1]
            qk_st[c] = r[2]
            kd_st[c] = r[3]
            gt_st[c] = jnp.broadcast_to(r[4], gt_st.shape[1:])

    levels = 0
    span = 1
    while 2 * span < CHUNK:
        levels += 1
        span *= 2
    local_stages = ([local_start] + [local_square, local_accumulate] * levels
                    + [functools.partial(local_solve, g) for g in range(len(groups))])
    rec_stages = [functools.partial(f, sub) for sub in range(nsub)
                  for f in (rec_read_state, rec_update)]
    for i in range(max(len(local_stages), len(rec_stages))):
        if i < len(local_stages):
            local_stages[i]()
        if i < len(rec_stages):
            rec_stages[i]()
    local_store()
    for b in range(nb):
        state_ref[b] = states[b]


def _gdn(qn, kn, vv, gcol, rows):
    batch, seq, _ = qn.shape
    nsteps = seq // rows
    nchains = batch * (rows // CHUNK)

    def cur(h, c):
        return (0, jnp.minimum(c, nsteps - 1), h)

    blk = pl.BlockSpec((batch, rows, LANES), cur)
    return pl.pallas_call(
        _gdn_kernel,
        grid=(GDN_HEADS, nsteps + 1),
        in_specs=[blk, blk, blk,
                  pl.BlockSpec((batch, rows, LANES), lambda h, c: (0, jnp.minimum(c, nsteps - 1), 0))],
        out_specs=pl.BlockSpec((batch, rows, LANES), lambda h, c: (0, jnp.maximum(c - 1, 0), h)),
        out_shape=jax.ShapeDtypeStruct(qn.shape, F32),
        scratch_shapes=[pltpu.VMEM((batch, GDN_DK, LANES), F32),
                        pltpu.VMEM((nchains, CHUNK, 2 * LANES), F32),
                        pltpu.VMEM((nchains, CHUNK, LANES), F32),
                        pltpu.VMEM((nchains, CHUNK, CHUNK), F32),
                        pltpu.VMEM((nchains, CHUNK, LANES), F32),
                        pltpu.VMEM((nchains, SUBLANES, LANES), F32)],
        compiler_params=_params(("parallel", "arbitrary")),
        name="gdn_chunks",
    )(qn, kn, vv, gcol)


def _merge_kernel(oa_ref, ob_ref, za_ref, zb_ref, ga_ref, gb_ref, x_ref, gng_ref, w_ref, fg_ref,
                  o_ref, mix_ref, *, final_norm):
    za = za_ref[...]
    ya = oa_ref[...] * (za * jax.nn.sigmoid(za))
    mix_ref[...] = (jax.nn.sigmoid(ga_ref[...]) * ya).astype(mix_ref.dtype)
    gng = gng_ref[...]
    for h in range(GDN_HEADS):
        sl = slice(h * LANES, (h + 1) * LANES)
        ob = ob_ref[:, sl]
        zb = zb_ref[:, sl]
        ms = jnp.mean(ob * ob, axis=-1, keepdims=True)
        yb = ob * lax.rsqrt(ms + NORM_EPS) * gng * (zb * jax.nn.sigmoid(zb))
        mix_ref[:, sl] += jax.nn.sigmoid(gb_ref[:, sl]) * yb
    y = x_ref[...] + jnp.dot(mix_ref[...].astype(MXU_DTYPE), w_ref[...], preferred_element_type=F32)
    if final_norm:
        ms = jnp.mean(y * y, axis=-1, keepdims=True)
        y = y * lax.rsqrt(ms + NORM_EPS) * fg_ref[...]
    o_ref[...] = y


def _merge_out(oa, ob, p, x2d, gdn_gain, w_out, final_gain, tm, final_norm):
    m = x2d.shape[0]

    def col(off):
        blk = off // D_MODEL
        return pl.BlockSpec((tm, D_MODEL), lambda i: (i, blk))

    row = pl.BlockSpec((tm, D_MODEL), lambda i: (i, 0))
    kern = functools.partial(_merge_kernel, final_norm=final_norm)
    return pl.pallas_call(
        kern,
        grid=(m // tm,),
        in_specs=[row, row, col(COL_ZA), col(COL_ZB), col(COL_GA), col(COL_GB), row,
                  pl.BlockSpec((1, LANES), lambda i: (0, 0)),
                  pl.BlockSpec((D_MODEL, D_MODEL), lambda i: (0, 0)),
                  pl.BlockSpec((1, D_MODEL), lambda i: (0, 0))],
        out_specs=row,
        out_shape=jax.ShapeDtypeStruct((m, D_MODEL), F32),
        scratch_shapes=[pltpu.VMEM((tm, D_MODEL), F32)],
        compiler_params=_params(("parallel",)),
        name="merge_out_proj",
    )(oa, ob, p, p, p, p, x2d, gdn_gain, w_out, final_gain)


def _pack_w_in(w):
    edges = [0]
    for n in IN_SPLITS:
        edges.append(edges[-1] + n)
    (qa, ka, va, za, qi, ki, wi, qkvb, zb, beta, a, gates) = [
        w[:, edges[j]:edges[j + 1]] for j in range(len(IN_SPLITS))]
    d = w.shape[0]
    qi = jnp.pad(qi.reshape(d, IDX_HEADS, IDX_DIM), ((0, 0), (0, 0), (0, LANES - IDX_DIM)))
    qi = qi.reshape(d, IDX_HEADS * LANES)
    ki = jnp.pad(ki, ((0, 0), (0, LANES - IDX_DIM)))
    misc = jnp.pad(jnp.concatenate([wi, beta, a], axis=1), ((0, 0), (0, LANES - 3 * 8)))
    packed = jnp.concatenate(
        [qa, za, zb, gates[:, :D_MODEL], gates[:, D_MODEL:], qkvb, qi, ka, va, ki, misc], axis=1)
    return packed.astype(MXU_DTYPE)


def _lane_pad(v, off):
    return jnp.pad(v.astype(F32), (off, LANES - off - v.shape[0])).reshape(1, LANES)


def _pick(n, prefs):
    for t in prefs:
        if n % t == 0:
            return t
    raise ValueError(f"no tile for {n}")


def kernel(x, positions, norm_gain, w_in, conv_w, a_log, dt_bias, gdn_norm_gain, idx_k_gain, w_out,
           final_gain):
    batch, seq, d = x.shape
    assert d == D_MODEL and seq % 128 == 0
    depth = w_in.shape[0]
    m = batch * seq
    tm_proj = _pick(m, (2048, 1024, 512, 256, 128))
    tn_proj = 768
    tm_prep = _pick(seq, (512, 256, 128))
    tm_merge = _pick(m, (256, 128))
    tq = 128
    tk = _pick(seq, (512, 256, 128))

    lane = jnp.arange(LANES)

    def inv_freq(half):
        rot = 2 * half
        f = ROPE_THETA ** (-jnp.arange(0, rot, 2, dtype=F32) / rot)
        return jnp.where(lane < rot, f[lane % half], 0.0).reshape(1, LANES).astype(F32)

    tabs = _rope_tables(positions.reshape(m, 1).astype(I32), inv_freq(ATTN_ROT_HALF),
                        inv_freq(IDX_ROT_HALF), tm_prep)
    tie_u = (jnp.arange(tk)[:, None] <= jnp.arange(tk)[None, :]).astype(BF16)

    x2d = x.reshape(m, d)
    for layer in range(depth):
        p = _project(x2d, norm_gain[layer].reshape(1, d), _pack_w_in(w_in[layer]), tm_proj, tn_proj)

        kgain = _lane_pad(idx_k_gain[layer], 0)
        q, k, v, qi, ki, wi = _attn_prep(p, tabs, kgain, tm_prep)
        shp = lambda t: t.reshape(batch, seq, t.shape[-1])
        o_a = _dsa(shp(q), shp(k), shp(v), shp(qi), shp(ki), shp(wi), tie_u, tq, tk).reshape(m, d)

        qn, kn, vv, gcol = _gdn_prep(p, conv_w[layer], _lane_pad(a_log[layer], MISC_A),
                                     _lane_pad(dt_bias[layer], MISC_A), tm_prep, seq)
        o_b = _gdn(shp(qn), shp(kn), shp(vv), shp(gcol), 8 * CHUNK).reshape(m, d)

        x2d = _merge_out(o_a, o_b, p, x2d, gdn_norm_gain[layer].reshape(1, LANES),
                         w_out[layer].astype(MXU_DTYPE), final_gain.reshape(1, d), tm_merge,
                         final_norm=(layer == depth - 1))
    return x2d.reshape(batch, seq, d)
```

```python
import functools
import math

import jax
import jax.numpy as jnp
from jax import lax
from jax.experimental import pallas as pl
from jax.experimental.pallas import tpu as pltpu

F32 = jnp.float32
BF16 = jnp.bfloat16
I32 = jnp.int32

D_MODEL = 1024
ATTN_HEADS = 8
ATTN_KV_HEADS = 2
HEAD_DIM = 128
ATTN_GROUP = ATTN_HEADS // ATTN_KV_HEADS
ATTN_ROT_HALF = 16
IDX_HEADS = 8
IDX_DIM = 64
IDX_ROT_HALF = 8
INDEX_TOPK = 256
ROPE_THETA = 500000.0
GDN_HEADS = 8
GDN_DK = 128
CONV_K = 4
CHUNK = 64
NORM_EPS = 1e-6
IN_SPLITS = (1024, 256, 256, 1024, 512, 64, 8, 3072, 1024, 8, 8, 2048)

LANES = 128
SUBLANES = 8
VMEM_LIMIT = 56 * 1024 * 1024

COL_QA, COL_ZA, COL_ZB, COL_GA, COL_GB = 0, 1024, 2048, 3072, 4096
COL_QB, COL_KB, COL_VB, COL_QI = 5120, 6144, 7168, 8192
COL_KA, COL_VA, COL_KI, COL_MISC = 9216, 9472, 9728, 9856
P_COLS = 9984
MISC_WI, MISC_BETA, MISC_A = 0, 8, 16

MXU_DTYPE = BF16
GDN_PRECISION = lax.Precision.HIGHEST

KEY_MIN = -2 ** 31
NEG_BIG = -1e30
QK_SCALE_LOG2 = HEAD_DIM ** -0.5 * math.log2(math.e)
COUNT_ROWS = 128
UNCHECKED_BITS = 16
BITS_PER_CHECK = 2
GDN_PACK = 2


def _params(sem):
    return pltpu.CompilerParams(dimension_semantics=sem, vmem_limit_bytes=VMEM_LIMIT)


def _proj_kernel(x_ref, gain_ref, w_ref, o_ref, h_ref):
    @pl.when(pl.program_id(1) == 0)
    def _():
        x = x_ref[...]
        ms = jnp.mean(x * x, axis=-1, keepdims=True)
        h_ref[...] = (x * lax.rsqrt(ms + NORM_EPS) * gain_ref[...]).astype(h_ref.dtype)

    o_ref[...] = jnp.dot(h_ref[...], w_ref[...], preferred_element_type=F32)


def _project(x2d, gain, w_packed, tm, tn):
    m = x2d.shape[0]
    return pl.pallas_call(
        _proj_kernel,
        grid=(m // tm, P_COLS // tn),
        in_specs=[
            pl.BlockSpec((tm, D_MODEL), lambda i, j: (i, 0)),
            pl.BlockSpec((1, D_MODEL), lambda i, j: (0, 0)),
            pl.BlockSpec((D_MODEL, tn), lambda i, j: (0, j)),
        ],
        out_specs=pl.BlockSpec((tm, tn), lambda i, j: (i, j)),
        out_shape=jax.ShapeDtypeStruct((m, P_COLS), F32),
        scratch_shapes=[pltpu.VMEM((tm, D_MODEL), MXU_DTYPE)],
        compiler_params=_params(("parallel", "arbitrary")),
        name="in_proj",
    )(x2d, gain, w_packed)


def _rope_tab_kernel(pos_ref, inva_ref, invi_ref, cosa_ref, sina_ref, cosi_ref, sini_ref):
    pos = pos_ref[...].astype(F32)
    lane = lax.broadcasted_iota(I32, (1, LANES), 1)

    def tables(inv, half, cos_ref, sin_ref):
        ang = pos * inv
        c = jnp.cos(ang)
        s = jnp.sin(ang)
        cos_ref[...] = jnp.where(lane < 2 * half, c, 1.0)
        sin_ref[...] = jnp.where(lane < half, -s, jnp.where(lane < 2 * half, s, 0.0))

    tables(inva_ref[...], ATTN_ROT_HALF, cosa_ref, sina_ref)
    tables(invi_ref[...], IDX_ROT_HALF, cosi_ref, sini_ref)


def _rope_tables(pos2d, inv_a, inv_i, tm):
    m = pos2d.shape[0]
    tab = jax.ShapeDtypeStruct((m, LANES), F32)
    row = pl.BlockSpec((tm, LANES), lambda i: (i, 0))
    one = pl.BlockSpec((1, LANES), lambda i: (0, 0))
    return pl.pallas_call(
        _rope_tab_kernel,
        grid=(m // tm,),
        in_specs=[pl.BlockSpec((tm, 1), lambda i: (i, 0)), one, one],
        out_specs=[row, row, row, row],
        out_shape=[tab, tab, tab, tab],
        compiler_params=_params(("parallel",)),
        name="rope_tables",
    )(pos2d, inv_a, inv_i)


def _rope(x, c, s, half):
    lane = lax.broadcasted_iota(I32, x.shape, 1)
    partner = jnp.where(lane < half, pltpu.roll(x, LANES - half, 1), pltpu.roll(x, half, 1))
    return x * c + partner * s


def _attn_prep_kernel(qa_ref, ka_ref, va_ref, qi_ref, ki_ref, misc_ref,
                      cosa_ref, sina_ref, cosi_ref, sini_ref, kgain_ref,
                      q_out, k_out, v_out, qi_out, ki_out, wi_out):
    ca, sa = cosa_ref[...], sina_ref[...]
    ci, si = cosi_ref[...], sini_ref[...]
    for h in range(ATTN_HEADS):
        sl = slice(h * LANES, (h + 1) * LANES)
        q_out[:, sl] = (_rope(qa_ref[:, sl], ca, sa, ATTN_ROT_HALF) * QK_SCALE_LOG2).astype(q_out.dtype)
    ones = jnp.ones((va_ref.shape[0], LANES), v_out.dtype)
    for h in range(ATTN_KV_HEADS):
        sl = slice(h * LANES, (h + 1) * LANES)
        k_out[:, sl] = _rope(ka_ref[:, sl], ca, sa, ATTN_ROT_HALF).astype(k_out.dtype)
        v_out[:, 2 * h * LANES:(2 * h + 1) * LANES] = va_ref[:, sl].astype(v_out.dtype)
        v_out[:, (2 * h + 1) * LANES:(2 * h + 2) * LANES] = ones
    for h in range(IDX_HEADS):
        sl = slice(h * LANES, (h + 1) * LANES)
        qi_out[:, sl] = _rope(qi_ref[:, sl], ci, si, IDX_ROT_HALF).astype(qi_out.dtype)
    ki = ki_ref[...]
    ms = jnp.sum(ki * ki, axis=-1, keepdims=True) * (1.0 / IDX_DIM)
    ki = ki * lax.rsqrt(ms + NORM_EPS) * kgain_ref[...]
    ki_out[...] = _rope(ki, ci, si, IDX_ROT_HALF).astype(ki_out.dtype)
    wi_out[...] = misc_ref[...] * (IDX_HEADS ** -0.5 * IDX_DIM ** -0.5)


def _attn_prep(p, tabs, kgain, tm):
    m = p.shape[0]
    cosa, sina, cosi, sini = tabs

    def col(width, off):
        blk = off // width
        return pl.BlockSpec((tm, width), lambda i: (i, blk))

    row = pl.BlockSpec((tm, LANES), lambda i: (i, 0))
    outs = [(ATTN_HEADS * LANES, MXU_DTYPE), (ATTN_KV_HEADS * LANES, MXU_DTYPE),
            (2 * ATTN_KV_HEADS * LANES, MXU_DTYPE), (IDX_HEADS * LANES, MXU_DTYPE),
            (LANES, MXU_DTYPE), (LANES, F32)]
    return pl.pallas_call(
        _attn_prep_kernel,
        grid=(m // tm,),
        in_specs=[col(1024, COL_QA), col(256, COL_KA), col(256, COL_VA), col(1024, COL_QI),
                  col(128, COL_KI), col(128, COL_MISC), row, row, row, row,
                  pl.BlockSpec((1, LANES), lambda i: (0, 0))],
        out_specs=[pl.BlockSpec((tm, w), lambda i: (i, 0)) for w, _ in outs],
        out_shape=[jax.ShapeDtypeStruct((m, w), d) for w, d in outs],
        compiler_params=_params(("parallel",)),
        name="attn_prep",
    )(p, p, p, p, p, p, cosa, sina, cosi, sini, kgain)


def _nt_dot(a, b):
    return lax.dot_general(a, b, (((1,), (1,)), ((), ())), preferred_element_type=F32)


def _dsa_kernel(q_ref, qi_ref, wi_ref, k_ref, v_ref, kidx_ref, u_ref, o_ref,
                keys_ref, m_ref, acc_ref, *, tq, tk, topk):
    i = pl.program_id(1)
    nkt = (i * tq + tq + tk - 1) // tk
    row_t = i * tq + lax.broadcasted_iota(I32, (tq, 1), 0)

    qi = qi_ref[...]
    wi = wi_ref[...]

    wb = [jnp.tile(jnp.broadcast_to(wi[:, MISC_WI + h:MISC_WI + h + 1], (tq, LANES)),
                   (1, tk // LANES)) for h in range(IDX_HEADS)]

    def index_tiles(kts):
        offs = [pl.multiple_of(kt * tk, tk) for kt in kts]
        raws = [[_nt_dot(qi[:, h * LANES:(h + 1) * LANES], kidx_ref[pl.ds(off, tk), :])
                 for h in range(IDX_HEADS)] for off in offs]
        for kt, off, raw in zip(kts, offs, raws):
            acc = jnp.maximum(raw[0], 0.0) * wb[0]
            for h in range(1, IDX_HEADS):
                acc = acc + jnp.maximum(raw[h], 0.0) * wb[h]
            bits = pltpu.bitcast(acc, I32)
            key = jnp.where(bits < 0, (bits ^ jnp.int32(0x7FFFFFFF)) + 1, bits)
            col = off + lax.broadcasted_iota(I32, (1, tk), 1)
            keys_ref[kt] = jnp.where(col <= row_t, key, jnp.int32(KEY_MIN))

    def idx_pair(j, carry):
        index_tiles([2 * j, 2 * j + 1])
        return carry

    lax.fori_loop(0, nkt // 2, idx_pair, 0)

    @pl.when(nkt % 2 == 1)
    def _():
        index_tiles([nkt - 1])

    def count_ge(cand):
        parts = []
        for r0 in range(0, tq, COUNT_ROWS):
            candb = jnp.broadcast_to(cand[r0:r0 + COUNT_ROWS], (COUNT_ROWS, LANES))

            def one(kt, acc, r0=r0, candb=candb):
                for c in range(tk // LANES):
                    kk = keys_ref[kt, r0:r0 + COUNT_ROWS, c * LANES:(c + 1) * LANES]
                    acc = acc + jnp.where(kk >= candb, 1.0, 0.0)
                return acc

            def two(j, acc, one=one):
                return one(2 * j + 1, one(2 * j, acc))

            acc = lax.fori_loop(0, nkt // 2, two, jnp.zeros((COUNT_ROWS, LANES), F32))
            acc = lax.cond(nkt % 2 == 1, functools.partial(one, nkt - 1), lambda a: a, acc)
            parts.append(jnp.sum(acc, axis=1, keepdims=True))
        return jnp.concatenate(parts, axis=0)

    kf = float(topk)

    def bit_step(b, ans, cnt):
        cand = ans + lax.shift_left(jnp.int32(1), 31 - b)
        cc = count_ge(cand)
        take = cc >= kf
        return jnp.where(take, cand, ans), jnp.where(take, cc, cnt)

    def bits_cond(c):
        b, _, cnt = c
        return (b < 32) & (jnp.max(jnp.abs(cnt - kf)) > 0.0)

    def bits_body(c):
        b, ans, cnt = c
        for u in range(BITS_PER_CHECK):
            ans, cnt = bit_step(b + u, ans, cnt)
        return b + BITS_PER_CHECK, ans, cnt

    thr, cnt = lax.fori_loop(
        0, UNCHECKED_BITS, lambda b, c: bit_step(b, *c),
        (jnp.full((tq, 1), KEY_MIN, I32), jnp.full((tq, 1), nkt * tk, I32).astype(F32)))
    _, thr, cnt = lax.while_loop(bits_cond, bits_body, (jnp.int32(UNCHECKED_BITS), thr, cnt))
    short = thr == jnp.int32(KEY_MIN)
    has_tie = jnp.max(jnp.where(short, 0.0, cnt - kf)) > 0.0

    q = q_ref[...]
    qg = [jnp.concatenate([q[:, (g * ATTN_GROUP + hh) * LANES:(g * ATTN_GROUP + hh + 1) * LANES]
                           for hh in range(ATTN_GROUP)], axis=0) for g in range(ATTN_KV_HEADS)]
    m_ref[...] = jnp.full(m_ref.shape, NEG_BIG, F32)
    acc_ref[...] = jnp.zeros(acc_ref.shape, F32)
    lane_tiles = tk // LANES

    def attend(tiles):
        units = [(pl.multiple_of(kt * tk, tk), bias, g) for kt, bias in tiles
                 for g in range(ATTN_KV_HEADS)]

        def qk(unit):
            off, _, g = unit
            return _nt_dot(qg[g], k_ref[pl.ds(off, tk), g * LANES:(g + 1) * LANES])

        ahead = 2
        scores = [qk(u) for u in units[:ahead]]
        for i, (off, bias, g) in enumerate(units):
            ps, alphas = [], []
            for hh in range(ATTN_GROUP):
                h = g * ATTN_GROUP + hh
                s = scores[i][hh * tq:(hh + 1) * tq] + bias
                m_prev = m_ref[h]
                m_new = jnp.maximum(m_prev, jnp.max(s, axis=1, keepdims=True))
                m_ref[h] = m_new
                alphas.append(jnp.exp2(m_prev - m_new))
                ps.append(jnp.exp2(s - jnp.tile(m_new, (1, lane_tiles))).astype(MXU_DTYPE))
            if i + ahead < len(units):
                scores.append(qk(units[i + ahead]))
            v_t = v_ref[pl.ds(off, tk), 2 * g * LANES:(2 * g + 2) * LANES]
            pv = jnp.dot(jnp.concatenate(ps, axis=0), v_t, preferred_element_type=F32)
            for hh in range(ATTN_GROUP):
                h = g * ATTN_GROUP + hh
                acc_ref[h] = (jnp.tile(alphas[hh], (1, 2)) * acc_ref[h]
                              + pv[hh * tq:(hh + 1) * tq])

    thr_floor = jnp.maximum(thr, jnp.int32(KEY_MIN + 1))

    def plain_bias(kt):
        return jnp.where(keys_ref[kt] >= thr_floor, 0.0, NEG_BIG)

    def pair_body(j, carry):
        attend([(2 * j, plain_bias(2 * j)), (2 * j + 1, plain_bias(2 * j + 1))])
        return carry

    def run_plain():
        lax.fori_loop(0, nkt // 2, pair_body, 0)

        @pl.when(nkt % 2 == 1)
        def _():
            attend([(nkt - 1, plain_bias(nkt - 1))])

    def run_ties():
        need = jnp.where(short, 0.0, kf - count_ge(thr + 1))

        def tie_body(kt, tie_carry):
            kk = keys_ref[kt]
            eq = kk == thr
            prefix = jnp.dot(jnp.where(eq, 1.0, 0.0).astype(BF16), u_ref[...],
                             preferred_element_type=F32)
            rank_ok = (prefix + tie_carry) <= need
            attend([(kt, jnp.where(kk > thr, 0.0,
                                   jnp.where(eq, jnp.where(rank_ok, 0.0, NEG_BIG), NEG_BIG)))])
            return tie_carry + prefix[:, tk - 1:tk]

        lax.fori_loop(0, nkt, tie_body, jnp.zeros((tq, 1), F32))

    lax.cond(has_tie, run_ties, run_plain)

    for h in range(ATTN_HEADS):
        acc = acc_ref[h]
        o_ref[:, h * LANES:(h + 1) * LANES] = (acc[:, :LANES] / acc[:, LANES:]).astype(o_ref.dtype)


def _dsa(q, k, v, qi, ki, wi, tie_u, tq, tk):
    b, s, _ = q.shape
    topk = min(INDEX_TOPK, s // 4)
    kern = functools.partial(_dsa_kernel, tq=tq, tk=tk, topk=topk)

    def qblk(w):
        return pl.BlockSpec((None, tq, w), lambda bi, i: (bi, i, 0))

    def full(w):
        return pl.BlockSpec((None, s, w), lambda bi, i: (bi, 0, 0))

    return pl.pallas_call(
        kern,
        grid=(b, s // tq),
        in_specs=[qblk(ATTN_HEADS * LANES), qblk(IDX_HEADS * LANES), qblk(LANES),
                  full(ATTN_KV_HEADS * LANES), full(2 * ATTN_KV_HEADS * LANES), full(LANES),
                  pl.BlockSpec((tk, tk), lambda bi, i: (0, 0))],
        out_specs=qblk(ATTN_HEADS * LANES),
        out_shape=jax.ShapeDtypeStruct((b, s, ATTN_HEADS * LANES), F32),
        scratch_shapes=[pltpu.VMEM((s // tk, tq, tk), I32),
                        pltpu.VMEM((ATTN_HEADS, tq, LANES), F32),
                        pltpu.VMEM((ATTN_HEADS, tq, 2 * LANES), F32)],
        compiler_params=_params(("parallel", "arbitrary")),
        name="dsa_attention",
    )(q, qi, wi, k, v, ki, tie_u)


def _shift_rows(x, halo, d):
    rolled = pltpu.roll(x, d, 0)
    head_rows = lax.broadcasted_iota(I32, (SUBLANES, x.shape[1]), 0)
    first = jnp.where(head_rows < d, pltpu.roll(halo, d, 0), rolled[:SUBLANES])
    return jnp.concatenate([first, rolled[SUBLANES:]], axis=0)


def _gdn_prep_kernel(q_ref, k_ref, v_ref, qh_ref, kh_ref, vh_ref, misc_ref,
                     wq_ref, wk_ref, wv_ref, alog_ref, dtb_ref,
                     qn_out, kn_out, vv_out, gcol_out, *, tm, seq):
    seq_start = (pl.program_id(0) * tm) % seq == 0

    def conv_silu(x_ref, halo_ref, w_ref):
        x = x_ref[...]
        halo = jnp.where(seq_start, 0.0, halo_ref[...])
        w = w_ref[...]
        y = x * w[CONV_K - 1:CONV_K]
        for d in range(1, CONV_K):
            y = y + _shift_rows(x, halo, d) * w[CONV_K - 1 - d:CONV_K - d]
        return y * jax.nn.sigmoid(y)

    def l2norm(x, out_ref, mult):
        for h in range(GDN_HEADS):
            sl = slice(h * LANES, (h + 1) * LANES)
            xh = x[:, sl]
            ss = jnp.sum(xh * xh, axis=-1, keepdims=True)
            out_ref[:, sl] = xh * (lax.rsqrt(ss + NORM_EPS) * mult)

    l2norm(conv_silu(q_ref, qh_ref, wq_ref), qn_out, GDN_DK ** -0.5)
    l2norm(conv_silu(k_ref, kh_ref, wk_ref), kn_out, 1.0)
    vv_out[...] = conv_silu(v_ref, vh_ref, wv_ref)

    misc = misc_ref[...]
    lane = lax.broadcasted_iota(I32, misc.shape, 1)
    row = lax.broadcasted_iota(I32, misc.shape, 0)
    is_a = (lane >= MISC_A) & (lane < MISC_A + GDN_HEADS)
    z = misc + dtb_ref[...]
    softplus = jnp.maximum(z, 0.0) + jnp.log(1.0 + jnp.exp(-jnp.abs(z)))
    g = jnp.where(is_a, -jnp.exp(alog_ref[...]) * softplus, 0.0)
    sh = 1
    while sh < CHUNK:
        g = g + jnp.where((row % CHUNK) >= sh, pltpu.roll(g, sh, 0), 0.0)
        sh *= 2
    is_beta = (lane >= MISC_BETA) & (lane < MISC_BETA + GDN_HEADS)
    gcol_out[...] = jnp.where(is_beta, jax.nn.sigmoid(misc), g)


def _gdn_prep(p, conv_w, alog_pad, dtb_pad, tm, seq):
    m = p.shape[0]
    wide = GDN_HEADS * LANES
    halo_blocks = tm // SUBLANES

    def col(off):
        blk = off // wide
        return pl.BlockSpec((tm, wide), lambda i: (i, blk))

    def halo(off):
        blk = off // wide
        return pl.BlockSpec((SUBLANES, wide), lambda i: (jnp.maximum(i * halo_blocks - 1, 0), blk))

    def wblk(j):
        return pl.BlockSpec((CONV_K, wide), lambda i: (0, j))

    one = pl.BlockSpec((1, LANES), lambda i: (0, 0))
    big = jax.ShapeDtypeStruct((m, wide), F32)
    kern = functools.partial(_gdn_prep_kernel, tm=tm, seq=seq)
    return pl.pallas_call(
        kern,
        grid=(m // tm,),
        in_specs=[col(COL_QB), col(COL_KB), col(COL_VB), halo(COL_QB), halo(COL_KB), halo(COL_VB),
                  pl.BlockSpec((tm, LANES), lambda i: (i, COL_MISC // LANES)),
                  wblk(0), wblk(1), wblk(2), one, one],
        out_specs=[pl.BlockSpec((tm, wide), lambda i: (i, 0))] * 3
        + [pl.BlockSpec((tm, LANES), lambda i: (i, 0))],
        out_shape=[big, big, big, jax.ShapeDtypeStruct((m, LANES), F32)],
        compiler_params=_params(("parallel",)),
        name="gdn_prep",
    )(p, p, p, p, p, p, p, conv_w, conv_w, conv_w, alog_pad, dtb_pad)


def _split(a):
    hi = a.astype(BF16)
    return hi, (a - hi.astype(F32)).astype(BF16)


def _dot3(a_split, b_split):
    (ah, al), (bh, bl) = a_split, b_split
    m = ah.shape[0]
    both = jnp.dot(jnp.concatenate([ah, al], axis=0), bh, preferred_element_type=F32)
    return both[:m] + both[m:] + jnp.dot(ah, bl, preferred_element_type=F32)


def _mxu_dot(a, b, dims=(((1,), (0,)), ((), ()))):
    return lax.dot_general(a.astype(MXU_DTYPE), b.astype(MXU_DTYPE), dims,
                           preferred_element_type=F32)


def _gdn_kernel(q_ref, k_ref, v_ref, gcol_ref, o_ref,
                state_ref, uw_st, qd_st, qk_st, kd_st, gt_st):
    h = pl.program_id(0)

    @pl.when(pl.program_id(1) == 0)
    def _():
        for ref in (state_ref, uw_st, qd_st, qk_st, kd_st, gt_st):
            ref[...] = jnp.zeros(ref.shape, F32)

    nb, rows = q_ref.shape[0], q_ref.shape[1]
    ii = lax.broadcasted_iota(I32, (CHUNK, CHUNK), 0)
    jj = lax.broadcasted_iota(I32, (CHUNK, CHUNK), 1)
    incl = ii >= jj
    strict = ii > jj
    eye = jnp.where(ii == jj, 1.0, 0.0)
    nt = (((1,), (1,)), ((), ()))
    tn = (((0,), (0,)), ((), ()))
    nsub = rows // CHUNK
    chains = [(b, sub) for b in range(nb) for sub in range(nsub)]

    states = [state_ref[b] for b in range(nb)]

    rec = {}

    def rec_read_state(sub):
        idx = [chains.index((b, sub)) for b in range(nb)]
        uws = [uw_st[c] for c in idx]
        rec["wss"] = [_mxu_dot(jnp.concatenate([uw[:, LANES:], qd_st[c]], axis=0), states[b])
                      for b, (c, uw) in enumerate(zip(idx, uws))]
        rec["v_news"] = [uw[:, :LANES] - ws[:CHUNK] for uw, ws in zip(uws, rec["wss"])]

    def rec_update(sub):
        sl = slice(sub * CHUNK, (sub + 1) * CHUNK)
        idx = [chains.index((b, sub)) for b in range(nb)]
        for b, c in enumerate(idx):
            o_ref[b, sl, :] = rec["wss"][b][CHUNK:] + _mxu_dot(qk_st[c], rec["v_news"][b])
        for b, c in enumerate(idx):
            states[b] = states[b] * gt_st[c][:1] + _mxu_dot(kd_st[c], rec["v_news"][b], tn)

    wide = GDN_PACK * CHUNK
    same_block = (lax.broadcasted_iota(I32, (wide, wide), 0) // CHUNK
                  == lax.broadcasted_iota(I32, (wide, wide), 1) // CHUNK)
    groups = [list(range(g, g + GDN_PACK)) for g in range(0, len(chains), GDN_PACK)]
    loc = {}

    def blockdiag(m):
        return jnp.where(same_block, jnp.tile(m, (GDN_PACK, 1)), jnp.zeros((), m.dtype))

    def packed_dot3(a_split, b_diag):
        (ah, al), (bh, bl) = a_split, b_diag
        both = jnp.dot(jnp.concatenate([ah, al], axis=0), bh, preferred_element_type=F32)
        return both[:CHUNK] + both[CHUNK:] + jnp.dot(ah, bl, preferred_element_type=F32)

    def local_start():
        col = {}
        for b in range(nb):
            gcol = gcol_ref[b]
            lane = lax.broadcasted_iota(I32, gcol.shape, 1)
            beta_all = jnp.sum(jnp.where(lane == MISC_BETA + h, gcol, 0.0), axis=1, keepdims=True)
            gc_all = jnp.sum(jnp.where(lane == MISC_A + h, gcol, 0.0), axis=1, keepdims=True)
            gc_sq = jnp.broadcast_to(gc_all, (rows, rows))
            col[b] = (beta_all, gc_all, gc_sq, gc_sq.T)
        xs, tinvs, rest = [], [], []
        for b, sub in chains:
            sl = slice(sub * CHUNK, (sub + 1) * CHUNK)
            beta_all, gc_all, gc_sq, gcr_sq = col[b]
            q, k, v = q_ref[b, sl, :], k_ref[b, sl, :], v_ref[b, sl, :]
            beta, gc = beta_all[sl], gc_all[sl]
            decay = jnp.where(incl, jnp.exp(gc_sq[sl, sl] - gcr_sq[sl, sl]), 0.0)
            kb = k * beta
            prods = _mxu_dot(jnp.concatenate([kb, q], axis=0), k, nt)
            x = -jnp.where(strict, prods[:CHUNK] * decay, 0.0)
            egc = jnp.exp(gc)
            g_last = gc[CHUNK - 1:CHUNK]
            xs.append(x)
            tinvs.append(eye + x)
            rest.append((jnp.concatenate([v * beta, kb * egc], axis=1), q * egc,
                         prods[CHUNK:] * decay, k * jnp.exp(g_last - gc), jnp.exp(g_last)))
        loc["rest"] = rest
        loc["xps"] = [_split(jnp.concatenate([xs[c] for c in grp], axis=1)) for grp in groups]
        loc["tps"] = [jnp.concatenate([tinvs[c] for c in grp], axis=1) for grp in groups]
        loc["diags"] = [(blockdiag(s[0]), blockdiag(s[1])) for s in loc["xps"]]

    def local_square():
        loc["xps"] = [_split(packed_dot3(s, d)) for s, d in zip(loc["xps"], loc["diags"])]
        loc["diags"] = [(blockdiag(s[0]), blockdiag(s[1])) for s in loc["xps"]]

    def local_accumulate():
        loc["tps"] = [t + packed_dot3(_split(t), d) for t, d in zip(loc["tps"], loc["diags"])]

    loc["uws"] = {}

    def local_solve(g):
        for j in range(GDN_PACK):
            c = groups[g][j]
            tinv = loc["tps"][g][:, j * CHUNK:(j + 1) * CHUNK]
            loc["uws"][c] = _dot3(_split(tinv), _split(loc["rest"][c][0]))

    def local_store():
        for c, r in enumerate(loc["rest"]):
            uw_st[c] = loc["uws"][c]
            qd_st[c] = r[1]
            qk_st[c] = r[2]
            kd_st[c] = r[3]
            gt_st[c] = jnp.broadcast_to(r[4], gt_st.shape[1:])

    levels = 0
    span = 1
    while 2 * span < CHUNK:
        levels += 1
        span *= 2
    local_stages = ([local_start] + [local_square, local_accumulate] * levels
                    + [functools.partial(local_solve, g) for g in range(len(groups))])
    rec_stages = [functools.partial(f, sub) for sub in range(nsub)
                  for f in (rec_read_state, rec_update)]
    for i in range(max(len(local_stages), len(rec_stages))):
        if i < len(local_stages):
            local_stages[i]()
        if i < len(rec_stages):
            rec_stages[i]()
    local_store()
    for b in range(nb):
        state_ref[b] = states[b]


def _gdn(qn, kn, vv, gcol, rows):
    batch, seq, _ = qn.shape
    nsteps = seq // rows
    nchains = batch * (rows // CHUNK)

    def cur(h, c):
        return (0, jnp.minimum(c, nsteps - 1), h)

    blk = pl.BlockSpec((batch, rows, LANES), cur)
    return pl.pallas_call(
        _gdn_kernel,
        grid=(GDN_HEADS, nsteps + 1),
        in_specs=[blk, blk, blk,
                  pl.BlockSpec((batch, rows, LANES), lambda h, c: (0, jnp.minimum(c, nsteps - 1), 0))],
        out_specs=pl.BlockSpec((batch, rows, LANES), lambda h, c: (0, jnp.maximum(c - 1, 0), h)),
        out_shape=jax.ShapeDtypeStruct(qn.shape, F32),
        scratch_shapes=[pltpu.VMEM((batch, GDN_DK, LANES), F32),
                        pltpu.VMEM((nchains, CHUNK, 2 * LANES), F32),
                        pltpu.VMEM((nchains, CHUNK, LANES), F32),
                        pltpu.VMEM((nchains, CHUNK, CHUNK), F32),
                        pltpu.VMEM((nchains, CHUNK, LANES), F32),
                        pltpu.VMEM((nchains, SUBLANES, LANES), F32)],
        compiler_params=_params(("parallel", "arbitrary")),
        name="gdn_chunks",
    )(qn, kn, vv, gcol)


def _merge_kernel(oa_ref, ob_ref, za_ref, zb_ref, ga_ref, gb_ref, x_ref, gng_ref, w_ref, fg_ref,
                  o_ref, mix_ref, *, final_norm):
    za = za_ref[...]
    ya = oa_ref[...] * (za * jax.nn.sigmoid(za))
    mix_ref[...] = (jax.nn.sigmoid(ga_ref[...]) * ya).astype(mix_ref.dtype)
    gng = gng_ref[...]
    for h in range(GDN_HEADS):
        sl = slice(h * LANES, (h + 1) * LANES)
        ob = ob_ref[:, sl]
        zb = zb_ref[:, sl]
        ms = jnp.mean(ob * ob, axis=-1, keepdims=True)
        yb = ob * lax.rsqrt(ms + NORM_EPS) * gng * (zb * jax.nn.sigmoid(zb))
        mix_ref[:, sl] += jax.nn.sigmoid(gb_ref[:, sl]) * yb
    y = x_ref[...] + jnp.dot(mix_ref[...].astype(MXU_DTYPE), w_ref[...], preferred_element_type=F32)
    if final_norm:
        ms = jnp.mean(y * y, axis=-1, keepdims=True)
        y = y * lax.rsqrt(ms + NORM_EPS) * fg_ref[...]
    o_ref[...] = y


def _merge_out(oa, ob, p, x2d, gdn_gain, w_out, final_gain, tm, final_norm):
    m = x2d.shape[0]

    def col(off):
        blk = off // D_MODEL
        return pl.BlockSpec((tm, D_MODEL), lambda i: (i, blk))

    row = pl.BlockSpec((tm, D_MODEL), lambda i: (i, 0))
    kern = functools.partial(_merge_kernel, final_norm=final_norm)
    return pl.pallas_call(
        kern,
        grid=(m // tm,),
        in_specs=[row, row, col(COL_ZA), col(COL_ZB), col(COL_GA), col(COL_GB), row,
                  pl.BlockSpec((1, LANES), lambda i: (0, 0)),
                  pl.BlockSpec((D_MODEL, D_MODEL), lambda i: (0, 0)),
                  pl.BlockSpec((1, D_MODEL), lambda i: (0, 0))],
        out_specs=row,
        out_shape=jax.ShapeDtypeStruct((m, D_MODEL), F32),
        scratch_shapes=[pltpu.VMEM((tm, D_MODEL), F32)],
        compiler_params=_params(("parallel",)),
        name="merge_out_proj",
    )(oa, ob, p, p, p, p, x2d, gdn_gain, w_out, final_gain)


def _pack_w_in(w):
    edges = [0]
    for n in IN_SPLITS:
        edges.append(edges[-1] + n)
    (qa, ka, va, za, qi, ki, wi, qkvb, zb, beta, a, gates) = [
        w[:, edges[j]:edges[j + 1]] for j in range(len(IN_SPLITS))]
    d = w.shape[0]
    qi = jnp.pad(qi.reshape(d, IDX_HEADS, IDX_DIM), ((0, 0), (0, 0), (0, LANES - IDX_DIM)))
    qi = qi.reshape(d, IDX_HEADS * LANES)
    ki = jnp.pad(ki, ((0, 0), (0, LANES - IDX_DIM)))
    misc = jnp.pad(jnp.concatenate([wi, beta, a], axis=1), ((0, 0), (0, LANES - 3 * 8)))
    packed = jnp.concatenate(
        [qa, za, zb, gates[:, :D_MODEL], gates[:, D_MODEL:], qkvb, qi, ka, va, ki, misc], axis=1)
    return packed.astype(MXU_DTYPE)


def _lane_pad(v, off):
    return jnp.pad(v.astype(F32), (off, LANES - off - v.shape[0])).reshape(1, LANES)


def _pick(n, prefs):
    for t in prefs:
        if n % t == 0:
            return t
    raise ValueError(f"no tile for {n}")


def kernel(x, positions, norm_gain, w_in, conv_w, a_log, dt_bias, gdn_norm_gain, idx_k_gain, w_out,
           final_gain):
    batch, seq, d = x.shape
    assert d == D_MODEL and seq % 128 == 0
    depth = w_in.shape[0]
    m = batch * seq
    tm_proj = _pick(m, (2048, 1024, 512, 256, 128))
    tn_proj = 768
    tm_prep = _pick(seq, (512, 256, 128))
    tm_merge = _pick(m, (256, 128))
    tq = 128
    tk = _pick(seq, (512, 256, 128))

    lane = jnp.arange(LANES)

    def inv_freq(half):
        rot = 2 * half
        f = ROPE_THETA ** (-jnp.arange(0, rot, 2, dtype=F32) / rot)
        return jnp.where(lane < rot, f[lane % half], 0.0).reshape(1, LANES).astype(F32)

    tabs = _rope_tables(positions.reshape(m, 1).astype(I32), inv_freq(ATTN_ROT_HALF),
                        inv_freq(IDX_ROT_HALF), tm_prep)
    tie_u = (jnp.arange(tk)[:, None] <= jnp.arange(tk)[None, :]).astype(BF16)

    x2d = x.reshape(m, d)
    for layer in range(depth):
        p = _project(x2d, norm_gain[layer].reshape(1, d), _pack_w_in(w_in[layer]), tm_proj, tn_proj)

        kgain = _lane_pad(idx_k_gain[layer], 0)
        q, k, v, qi, ki, wi = _attn_prep(p, tabs, kgain, tm_prep)
        shp = lambda t: t.reshape(batch, seq, t.shape[-1])
        o_a = _dsa(shp(q), shp(k), shp(v), shp(qi), shp(ki), shp(wi), tie_u, tq, tk).reshape(m, d)

        qn, kn, vv, gcol = _gdn_prep(p, conv_w[layer], _lane_pad(a_log[layer], MISC_A),
                                     _lane_pad(dt_bias[layer], MISC_A), tm_prep, seq)
        o_b = _gdn(shp(qn), shp(kn), shp(vv), shp(gcol), 8 * CHUNK).reshape(m, d)

        x2d = _merge_out(o_a, o_b, p, x2d, gdn_norm_gain[layer].reshape(1, LANES),
                         w_out[layer].astype(MXU_DTYPE), final_gain.reshape(1, d), tm_merge,
                         final_norm=(layer == depth - 1))
    return x2d.reshape(batch, seq, d)
```

```python
import functools
import math

import jax
import jax.numpy as jnp
from jax import lax
from jax.experimental import pallas as pl
from jax.experimental.pallas import tpu as pltpu

F32 = jnp.float32
BF16 = jnp.bfloat16
I32 = jnp.int32

D_MODEL = 1024
ATTN_HEADS = 8
ATTN_KV_HEADS = 2
HEAD_DIM = 128
ATTN_GROUP = ATTN_HEADS // ATTN_KV_HEADS
ATTN_ROT_HALF = 16
IDX_HEADS = 8
IDX_DIM = 64
IDX_ROT_HALF = 8
INDEX_TOPK = 256
ROPE_THETA = 500000.0
GDN_HEADS = 8
GDN_DK = 128
CONV_K = 4
CHUNK = 64
NORM_EPS = 1e-6
IN_SPLITS = (1024, 256, 256, 1024, 512, 64, 8, 3072, 1024, 8, 8, 2048)

LANES = 128
SUBLANES = 8
VMEM_LIMIT = 56 * 1024 * 1024

COL_QA, COL_ZA, COL_ZB, COL_GA, COL_GB = 0, 1024, 2048, 3072, 4096
COL_QB, COL_KB, COL_VB, COL_QI = 5120, 6144, 7168, 8192
P_MAIN_COLS = 9216
TAIL_KA, TAIL_VA, TAIL_KI, TAIL_MISC = 0, 256, 512, 640
P_TAIL_COLS = 768
P_COLS = P_MAIN_COLS + P_TAIL_COLS
P_MAIN_DTYPE = BF16
MISC_WI, MISC_BETA, MISC_A = 0, 8, 16

MXU_DTYPE = BF16
GDN_PRECISION = lax.Precision.HIGHEST

KEY_MIN = -2 ** 31
NEG_BIG = -1e30
QK_SCALE_LOG2 = HEAD_DIM ** -0.5 * math.log2(math.e)
COUNT_ROWS = 128
UNCHECKED_BITS = 16
BITS_PER_CHECK = 2
HALO_ROWS = 16
GDN_PACK = 2


def _params(sem):
    return pltpu.CompilerParams(dimension_semantics=sem, vmem_limit_bytes=VMEM_LIMIT)


def _proj_kernel(x_ref, gain_ref, w_ref, main_ref, tail_ref, h_ref, *, main_tiles):
    j = pl.program_id(1)

    @pl.when(j == 0)
    def _():
        x = x_ref[...]
        ms = jnp.mean(x * x, axis=-1, keepdims=True)
        h_ref[...] = (x * lax.rsqrt(ms + NORM_EPS) * gain_ref[...]).astype(h_ref.dtype)

    acc = jnp.dot(h_ref[...], w_ref[...], preferred_element_type=F32)

    @pl.when(j < main_tiles)
    def _():
        main_ref[...] = acc.astype(main_ref.dtype)

    @pl.when(j >= main_tiles)
    def _():
        tail_ref[...] = acc


def _project(x2d, gain, w_packed, tm, tn):
    m = x2d.shape[0]
    assert P_MAIN_COLS % tn == 0 and P_TAIL_COLS == tn
    main_tiles = P_MAIN_COLS // tn
    return pl.pallas_call(
        functools.partial(_proj_kernel, main_tiles=main_tiles),
        grid=(m // tm, main_tiles + 1),
        in_specs=[
            pl.BlockSpec((tm, D_MODEL), lambda i, j: (i, 0)),
            pl.BlockSpec((1, D_MODEL), lambda i, j: (0, 0)),
            pl.BlockSpec((D_MODEL, tn), lambda i, j: (0, j)),
        ],
        out_specs=[pl.BlockSpec((tm, tn), lambda i, j: (i, jnp.minimum(j, main_tiles - 1))),
                   pl.BlockSpec((tm, tn), lambda i, j: (i, 0))],
        out_shape=[jax.ShapeDtypeStruct((m, P_MAIN_COLS), P_MAIN_DTYPE),
                   jax.ShapeDtypeStruct((m, P_TAIL_COLS), F32)],
        scratch_shapes=[pltpu.VMEM((tm, D_MODEL), MXU_DTYPE)],
        compiler_params=_params(("parallel", "arbitrary")),
        name="in_proj",
    )(x2d, gain, w_packed)


def _rope_tab_kernel(pos_ref, inva_ref, invi_ref, cosa_ref, sina_ref, cosi_ref, sini_ref):
    pos = pos_ref[...].astype(F32)
    lane = lax.broadcasted_iota(I32, (1, LANES), 1)

    def tables(inv, half, cos_ref, sin_ref):
        ang = pos * inv
        c = jnp.cos(ang)
        s = jnp.sin(ang)
        cos_ref[...] = jnp.where(lane < 2 * half, c, 1.0)
        sin_ref[...] = jnp.where(lane < half, -s, jnp.where(lane < 2 * half, s, 0.0))

    tables(inva_ref[...], ATTN_ROT_HALF, cosa_ref, sina_ref)
    tables(invi_ref[...], IDX_ROT_HALF, cosi_ref, sini_ref)


def _rope_tables(pos2d, inv_a, inv_i, tm):
    m = pos2d.shape[0]
    tab = jax.ShapeDtypeStruct((m, LANES), F32)
    row = pl.BlockSpec((tm, LANES), lambda i: (i, 0))
    one = pl.BlockSpec((1, LANES), lambda i: (0, 0))
    return pl.pallas_call(
        _rope_tab_kernel,
        grid=(m // tm,),
        in_specs=[pl.BlockSpec((tm, 1), lambda i: (i, 0)), one, one],
        out_specs=[row, row, row, row],
        out_shape=[tab, tab, tab, tab],
        compiler_params=_params(("parallel",)),
        name="rope_tables",
    )(pos2d, inv_a, inv_i)


def _rope(x, c, s, half):
    lane = lax.broadcasted_iota(I32, x.shape, 1)
    partner = jnp.where(lane < half, pltpu.roll(x, LANES - half, 1), pltpu.roll(x, half, 1))
    return x * c + partner * s


def _attn_prep_kernel(qa_ref, ka_ref, va_ref, qi_ref, ki_ref, misc_ref,
                      cosa_ref, sina_ref, cosi_ref, sini_ref, kgain_ref,
                      q_out, k_out, v_out, qi_out, ki_out, wi_out):
    ca, sa = cosa_ref[...], sina_ref[...]
    ci, si = cosi_ref[...], sini_ref[...]
    for h in range(ATTN_HEADS):
        sl = slice(h * LANES, (h + 1) * LANES)
        q_out[:, sl] = (_rope(qa_ref[:, sl].astype(F32), ca, sa, ATTN_ROT_HALF)
                        * QK_SCALE_LOG2).astype(q_out.dtype)
    ones = jnp.ones((va_ref.shape[0], LANES), v_out.dtype)
    for h in range(ATTN_KV_HEADS):
        sl = slice(h * LANES, (h + 1) * LANES)
        k_out[:, sl] = _rope(ka_ref[:, sl], ca, sa, ATTN_ROT_HALF).astype(k_out.dtype)
        v_out[:, 2 * h * LANES:(2 * h + 1) * LANES] = va_ref[:, sl].astype(v_out.dtype)
        v_out[:, (2 * h + 1) * LANES:(2 * h + 2) * LANES] = ones
    for h in range(IDX_HEADS):
        sl = slice(h * LANES, (h + 1) * LANES)
        qi_out[:, sl] = _rope(qi_ref[:, sl].astype(F32), ci, si, IDX_ROT_HALF).astype(qi_out.dtype)
    ki = ki_ref[...]
    ms = jnp.sum(ki * ki, axis=-1, keepdims=True) * (1.0 / IDX_DIM)
    ki = ki * lax.rsqrt(ms + NORM_EPS) * kgain_ref[...]
    ki_out[...] = _rope(ki, ci, si, IDX_ROT_HALF).astype(ki_out.dtype)
    wi_out[...] = misc_ref[...] * (IDX_HEADS ** -0.5 * IDX_DIM ** -0.5)


def _attn_prep(p_main, p_tail, tabs, kgain, tm):
    m = p_main.shape[0]
    cosa, sina, cosi, sini = tabs

    def col(width, off):
        blk = off // width
        return pl.BlockSpec((tm, width), lambda i: (i, blk))

    row = pl.BlockSpec((tm, LANES), lambda i: (i, 0))
    outs = [(ATTN_HEADS * LANES, MXU_DTYPE), (ATTN_KV_HEADS * LANES, MXU_DTYPE),
            (2 * ATTN_KV_HEADS * LANES, MXU_DTYPE), (IDX_HEADS * LANES, MXU_DTYPE),
            (LANES, MXU_DTYPE), (LANES, F32)]
    return pl.pallas_call(
        _attn_prep_kernel,
        grid=(m // tm,),
        in_specs=[col(1024, COL_QA), col(256, TAIL_KA), col(256, TAIL_VA), col(1024, COL_QI),
                  col(128, TAIL_KI), col(128, TAIL_MISC), row, row, row, row,
                  pl.BlockSpec((1, LANES), lambda i: (0, 0))],
        out_specs=[pl.BlockSpec((tm, w), lambda i: (i, 0)) for w, _ in outs],
        out_shape=[jax.ShapeDtypeStruct((m, w), d) for w, d in outs],
        compiler_params=_params(("parallel",)),
        name="attn_prep",
    )(p_main, p_tail, p_tail, p_main, p_tail, p_tail, cosa, sina, cosi, sini, kgain)


def _nt_dot(a, b):
    return lax.dot_general(a, b, (((1,), (1,)), ((), ())), preferred_element_type=F32)


def _dsa_kernel(q_ref, qi_ref, wi_ref, k_ref, v_ref, kidx_ref, u_ref, o_ref,
                keys_ref, m_ref, acc_ref, *, tq, tk, topk):
    i = pl.program_id(1)
    nkt = (i * tq + tq + tk - 1) // tk
    row_t = i * tq + lax.broadcasted_iota(I32, (tq, 1), 0)

    qi = qi_ref[...]
    wi = wi_ref[...]

    wb = [jnp.tile(jnp.broadcast_to(wi[:, MISC_WI + h:MISC_WI + h + 1], (tq, LANES)),
                   (1, tk // LANES)) for h in range(IDX_HEADS)]

    def index_tiles(kts):
        offs = [pl.multiple_of(kt * tk, tk) for kt in kts]
        raws = [[_nt_dot(qi[:, h * LANES:(h + 1) * LANES], kidx_ref[pl.ds(off, tk), :])
                 for h in range(IDX_HEADS)] for off in offs]
        for kt, off, raw in zip(kts, offs, raws):
            acc = jnp.maximum(raw[0], 0.0) * wb[0]
            for h in range(1, IDX_HEADS):
                acc = acc + jnp.maximum(raw[h], 0.0) * wb[h]
            bits = pltpu.bitcast(acc, I32)
            key = jnp.where(bits < 0, (bits ^ jnp.int32(0x7FFFFFFF)) + 1, bits)
            col = off + lax.broadcasted_iota(I32, (1, tk), 1)
            keys_ref[kt] = jnp.where(col <= row_t, key, jnp.int32(KEY_MIN))

    def idx_pair(j, carry):
        index_tiles([2 * j, 2 * j + 1])
        return carry

    lax.fori_loop(0, nkt // 2, idx_pair, 0)

    @pl.when(nkt % 2 == 1)
    def _():
        index_tiles([nkt - 1])

    def count_ge(cand):
        parts = []
        for r0 in range(0, tq, COUNT_ROWS):
            candb = jnp.broadcast_to(cand[r0:r0 + COUNT_ROWS], (COUNT_ROWS, LANES))

            def one(kt, acc, r0=r0, candb=candb):
                for c in range(tk // LANES):
                    kk = keys_ref[kt, r0:r0 + COUNT_ROWS, c * LANES:(c + 1) * LANES]
                    acc = acc + jnp.where(kk >= candb, 1.0, 0.0)
                return acc

            def two(j, acc, one=one):
                return one(2 * j + 1, one(2 * j, acc))

            acc = lax.fori_loop(0, nkt // 2, two, jnp.zeros((COUNT_ROWS, LANES), F32))
            acc = lax.cond(nkt % 2 == 1, functools.partial(one, nkt - 1), lambda a: a, acc)
            parts.append(jnp.sum(acc, axis=1, keepdims=True))
        return jnp.concatenate(parts, axis=0)

    kf = float(topk)

    def bit_step(b, ans, cnt):
        cand = ans + lax.shift_left(jnp.int32(1), 31 - b)
        cc = count_ge(cand)
        take = cc >= kf
        return jnp.where(take, cand, ans), jnp.where(take, cc, cnt)

    def bits_cond(c):
        b, _, cnt = c
        return (b < 32) & (jnp.max(jnp.abs(cnt - kf)) > 0.0)

    def bits_body(c):
        b, ans, cnt = c
        for u in range(BITS_PER_CHECK):
            ans, cnt = bit_step(b + u, ans, cnt)
        return b + BITS_PER_CHECK, ans, cnt

    thr, cnt = lax.fori_loop(
        0, UNCHECKED_BITS, lambda b, c: bit_step(b, *c),
        (jnp.full((tq, 1), KEY_MIN, I32), jnp.full((tq, 1), nkt * tk, I32).astype(F32)))
    _, thr, cnt = lax.while_loop(bits_cond, bits_body, (jnp.int32(UNCHECKED_BITS), thr, cnt))
    short = thr == jnp.int32(KEY_MIN)
    has_tie = jnp.max(jnp.where(short, 0.0, cnt - kf)) > 0.0

    q = q_ref[...]
    qg = [jnp.concatenate([q[:, (g * ATTN_GROUP + hh) * LANES:(g * ATTN_GROUP + hh + 1) * LANES]
                           for hh in range(ATTN_GROUP)], axis=0) for g in range(ATTN_KV_HEADS)]
    m_ref[...] = jnp.full(m_ref.shape, NEG_BIG, F32)
    acc_ref[...] = jnp.zeros(acc_ref.shape, F32)
    lane_tiles = tk // LANES

    def attend(tiles):
        units = [(pl.multiple_of(kt * tk, tk), bias, g) for kt, bias in tiles
                 for g in range(ATTN_KV_HEADS)]

        def qk(unit):
            off, _, g = unit
            return _nt_dot(qg[g], k_ref[pl.ds(off, tk), g * LANES:(g + 1) * LANES])

        ahead = 2
        scores = [qk(u) for u in units[:ahead]]
        for i, (off, bias, g) in enumerate(units):
            ps, alphas = [], []
            for hh in range(ATTN_GROUP):
                h = g * ATTN_GROUP + hh
                s = scores[i][hh * tq:(hh + 1) * tq] + bias
                m_prev = m_ref[h]
                m_new = jnp.maximum(m_prev, jnp.max(s, axis=1, keepdims=True))
                m_ref[h] = m_new
                alphas.append(jnp.exp2(m_prev - m_new))
                ps.append(jnp.exp2(s - jnp.tile(m_new, (1, lane_tiles))).astype(MXU_DTYPE))
            if i + ahead < len(units):
                scores.append(qk(units[i + ahead]))
            v_t = v_ref[pl.ds(off, tk), 2 * g * LANES:(2 * g + 2) * LANES]
            pv = jnp.dot(jnp.concatenate(ps, axis=0), v_t, preferred_element_type=F32)
            for hh in range(ATTN_GROUP):
                h = g * ATTN_GROUP + hh
                acc_ref[h] = (jnp.tile(alphas[hh], (1, 2)) * acc_ref[h]
                              + pv[hh * tq:(hh + 1) * tq])

    thr_floor = jnp.maximum(thr, jnp.int32(KEY_MIN + 1))

    def plain_bias(kt):
        return jnp.where(keys_ref[kt] >= thr_floor, 0.0, NEG_BIG)

    def pair_body(j, carry):
        attend([(2 * j, plain_bias(2 * j)), (2 * j + 1, plain_bias(2 * j + 1))])
        return carry

    def run_plain():
        lax.fori_loop(0, nkt // 2, pair_body, 0)

        @pl.when(nkt % 2 == 1)
        def _():
            attend([(nkt - 1, plain_bias(nkt - 1))])

    def run_ties():
        need = jnp.where(short, 0.0, kf - count_ge(thr + 1))

        def tie_body(kt, tie_carry):
            kk = keys_ref[kt]
            eq = kk == thr
            prefix = jnp.dot(jnp.where(eq, 1.0, 0.0).astype(BF16), u_ref[...],
                             preferred_element_type=F32)
            rank_ok = (prefix + tie_carry) <= need
            attend([(kt, jnp.where(kk > thr, 0.0,
                                   jnp.where(eq, jnp.where(rank_ok, 0.0, NEG_BIG), NEG_BIG)))])
            return tie_carry + prefix[:, tk - 1:tk]

        lax.fori_loop(0, nkt, tie_body, jnp.zeros((tq, 1), F32))

    lax.cond(has_tie, run_ties, run_plain)

    for h in range(ATTN_HEADS):
        acc = acc_ref[h]
        o_ref[:, h * LANES:(h + 1) * LANES] = (acc[:, :LANES] / acc[:, LANES:]).astype(o_ref.dtype)


def _dsa(q, k, v, qi, ki, wi, tie_u, tq, tk):
    b, s, _ = q.shape
    topk = min(INDEX_TOPK, s // 4)
    kern = functools.partial(_dsa_kernel, tq=tq, tk=tk, topk=topk)

    def qblk(w):
        return pl.BlockSpec((None, tq, w), lambda bi, i: (bi, i, 0))

    def full(w):
        return pl.BlockSpec((None, s, w), lambda bi, i: (bi, 0, 0))

    return pl.pallas_call(
        kern,
        grid=(b, s // tq),
        in_specs=[qblk(ATTN_HEADS * LANES), qblk(IDX_HEADS * LANES), qblk(LANES),
                  full(ATTN_KV_HEADS * LANES), full(2 * ATTN_KV_HEADS * LANES), full(LANES),
                  pl.BlockSpec((tk, tk), lambda bi, i: (0, 0))],
        out_specs=qblk(ATTN_HEADS * LANES),
        out_shape=jax.ShapeDtypeStruct((b, s, ATTN_HEADS * LANES), F32),
        scratch_shapes=[pltpu.VMEM((s // tk, tq, tk), I32),
                        pltpu.VMEM((ATTN_HEADS, tq, LANES), F32),
                        pltpu.VMEM((ATTN_HEADS, tq, 2 * LANES), F32)],
        compiler_params=_params(("parallel", "arbitrary")),
        name="dsa_attention",
    )(q, qi, wi, k, v, ki, tie_u)


def _shift_rows(x, halo, d):
    rolled = pltpu.roll(x, d, 0)
    head_rows = lax.broadcasted_iota(I32, (SUBLANES, x.shape[1]), 0)
    first = jnp.where(head_rows < d, pltpu.roll(halo, d, 0), rolled[:SUBLANES])
    return jnp.concatenate([first, rolled[SUBLANES:]], axis=0)


def _gdn_prep_kernel(q_ref, k_ref, v_ref, qh_ref, kh_ref, vh_ref, misc_ref,
                     wq_ref, wk_ref, wv_ref, alog_ref, dtb_ref,
                     qn_out, kn_out, vv_out, gcol_out, *, tm, seq):
    seq_start = (pl.program_id(0) * tm) % seq == 0

    def conv_silu(x_ref, halo_ref, w_ref):
        x = x_ref[...].astype(F32)
        halo = halo_ref[...].astype(F32)[HALO_ROWS - SUBLANES:]
        halo = jnp.where(seq_start, 0.0, halo)
        w = w_ref[...]
        y = x * w[CONV_K - 1:CONV_K]
        for d in range(1, CONV_K):
            y = y + _shift_rows(x, halo, d) * w[CONV_K - 1 - d:CONV_K - d]
        return y * jax.nn.sigmoid(y)

    def l2norm(x, out_ref, mult):
        for h in range(GDN_HEADS):
            sl = slice(h * LANES, (h + 1) * LANES)
            xh = x[:, sl]
            ss = jnp.sum(xh * xh, axis=-1, keepdims=True)
            out_ref[:, sl] = xh * (lax.rsqrt(ss + NORM_EPS) * mult)

    l2norm(conv_silu(q_ref, qh_ref, wq_ref), qn_out, GDN_DK ** -0.5)
    l2norm(conv_silu(k_ref, kh_ref, wk_ref), kn_out, 1.0)
    vv_out[...] = conv_silu(v_ref, vh_ref, wv_ref)

    misc = misc_ref[...]
    lane = lax.broadcasted_iota(I32, misc.shape, 1)
    row = lax.broadcasted_iota(I32, misc.shape, 0)
    is_a = (lane >= MISC_A) & (lane < MISC_A + GDN_HEADS)
    z = misc + dtb_ref[...]
    softplus = jnp.maximum(z, 0.0) + jnp.log(1.0 + jnp.exp(-jnp.abs(z)))
    g = jnp.where(is_a, -jnp.exp(alog_ref[...]) * softplus, 0.0)
    sh = 1
    while sh < CHUNK:
        g = g + jnp.where((row % CHUNK) >= sh, pltpu.roll(g, sh, 0), 0.0)
        sh *= 2
    is_beta = (lane >= MISC_BETA) & (lane < MISC_BETA + GDN_HEADS)
    gcol_out[...] = jnp.where(is_beta, jax.nn.sigmoid(misc), g)


def _gdn_prep(p_main, p_tail, conv_w, alog_pad, dtb_pad, tm, seq):
    m = p_main.shape[0]
    wide = GDN_HEADS * LANES
    halo_blocks = tm // HALO_ROWS

    def col(off):
        blk = off // wide
        return pl.BlockSpec((tm, wide), lambda i: (i, blk))

    def halo(off):
        blk = off // wide
        return pl.BlockSpec((HALO_ROWS, wide), lambda i: (jnp.maximum(i * halo_blocks - 1, 0), blk))

    def wblk(j):
        return pl.BlockSpec((CONV_K, wide), lambda i: (0, j))

    one = pl.BlockSpec((1, LANES), lambda i: (0, 0))
    big = jax.ShapeDtypeStruct((m, wide), F32)
    kern = functools.partial(_gdn_prep_kernel, tm=tm, seq=seq)
    return pl.pallas_call(
        kern,
        grid=(m // tm,),
        in_specs=[col(COL_QB), col(COL_KB), col(COL_VB), halo(COL_QB), halo(COL_KB), halo(COL_VB),
                  pl.BlockSpec((tm, LANES), lambda i: (i, TAIL_MISC // LANES)),
                  wblk(0), wblk(1), wblk(2), one, one],
        out_specs=[pl.BlockSpec((tm, wide), lambda i: (i, 0))] * 3
        + [pl.BlockSpec((tm, LANES), lambda i: (i, 0))],
        out_shape=[big, big, big, jax.ShapeDtypeStruct((m, LANES), F32)],
        compiler_params=_params(("parallel",)),
        name="gdn_prep",
    )(p_main, p_main, p_main, p_main, p_main, p_main, p_tail, conv_w, conv_w, conv_w,
      alog_pad, dtb_pad)


def _split(a):
    hi = a.astype(BF16)
    return hi, (a - hi.astype(F32)).astype(BF16)


def _dot3(a_split, b_split):
    (ah, al), (bh, bl) = a_split, b_split
    m = ah.shape[0]
    both = jnp.dot(jnp.concatenate([ah, al], axis=0), bh, preferred_element_type=F32)
    return both[:m] + both[m:] + jnp.dot(ah, bl, preferred_element_type=F32)


def _mxu_dot(a, b, dims=(((1,), (0,)), ((), ()))):
    return lax.dot_general(a.astype(MXU_DTYPE), b.astype(MXU_DTYPE), dims,
                           preferred_element_type=F32)


def _gdn_kernel(q_ref, k_ref, v_ref, gcol_ref, o_ref,
                state_ref, uw_st, qd_st, qk_st, kd_st, gt_st):
    h = pl.program_id(0)

    @pl.when(pl.program_id(1) == 0)
    def _():
        for ref in (state_ref, uw_st, qd_st, qk_st, kd_st, gt_st):
            ref[...] = jnp.zeros(ref.shape, F32)

    nb, rows = q_ref.shape[0], q_ref.shape[1]
    ii = lax.broadcasted_iota(I32, (CHUNK, CHUNK), 0)
    jj = lax.broadcasted_iota(I32, (CHUNK, CHUNK), 1)
    incl = ii >= jj
    strict = ii > jj
    eye = jnp.where(ii == jj, 1.0, 0.0)
    nt = (((1,), (1,)), ((), ()))
    tn = (((0,), (0,)), ((), ()))
    nsub = rows // CHUNK
    chains = [(b, sub) for b in range(nb) for sub in range(nsub)]

    states = [state_ref[b] for b in range(nb)]

    rec = {}

    def rec_read_state(sub):
        idx = [chains.index((b, sub)) for b in range(nb)]
        uws = [uw_st[c] for c in idx]
        rec["wss"] = [_mxu_dot(jnp.concatenate([uw[:, LANES:], qd_st[c]], axis=0), states[b])
                      for b, (c, uw) in enumerate(zip(idx, uws))]
        rec["v_news"] = [uw[:, :LANES] - ws[:CHUNK] for uw, ws in zip(uws, rec["wss"])]

    def rec_update(sub):
        sl = slice(sub * CHUNK, (sub + 1) * CHUNK)
        idx = [chains.index((b, sub)) for b in range(nb)]
        for b, c in enumerate(idx):
            o_ref[b, sl, :] = rec["wss"][b][CHUNK:] + _mxu_dot(qk_st[c], rec["v_news"][b])
        for b, c in enumerate(idx):
            states[b] = states[b] * gt_st[c][:1] + _mxu_dot(kd_st[c], rec["v_news"][b], tn)

    wide = GDN_PACK * CHUNK
    same_block = (lax.broadcasted_iota(I32, (wide, wide), 0) // CHUNK
                  == lax.broadcasted_iota(I32, (wide, wide), 1) // CHUNK)
    groups = [list(range(g, g + GDN_PACK)) for g in range(0, len(chains), GDN_PACK)]
    loc = {}

    def blockdiag(m):
        return jnp.where(same_block, jnp.tile(m, (GDN_PACK, 1)), jnp.zeros((), m.dtype))

    def packed_dot3(a_split, b_diag):
        (ah, al), (bh, bl) = a_split, b_diag
        both = jnp.dot(jnp.concatenate([ah, al], axis=0), bh, preferred_element_type=F32)
        return both[:CHUNK] + both[CHUNK:] + jnp.dot(ah, bl, preferred_element_type=F32)

    def local_start():
        col = {}
        for b in range(nb):
            gcol = gcol_ref[b]
            lane = lax.broadcasted_iota(I32, gcol.shape, 1)
            beta_all = jnp.sum(jnp.where(lane == MISC_BETA + h, gcol, 0.0), axis=1, keepdims=True)
            gc_all = jnp.sum(jnp.where(lane == MISC_A + h, gcol, 0.0), axis=1, keepdims=True)
            gc_sq = jnp.broadcast_to(gc_all, (rows, rows))
            col[b] = (beta_all, gc_all, gc_sq, gc_sq.T)
        xs, tinvs, rest = [], [], []
        for b, sub in chains:
            sl = slice(sub * CHUNK, (sub + 1) * CHUNK)
            beta_all, gc_all, gc_sq, gcr_sq = col[b]
            q, k, v = q_ref[b, sl, :], k_ref[b, sl, :], v_ref[b, sl, :]
            beta, gc = beta_all[sl], gc_all[sl]
            decay = jnp.where(incl, jnp.exp(gc_sq[sl, sl] - gcr_sq[sl, sl]), 0.0)
            kb = k * beta
            prods = _mxu_dot(jnp.concatenate([kb, q], axis=0), k, nt)
            x = -jnp.where(strict, prods[:CHUNK] * decay, 0.0)
            egc = jnp.exp(gc)
            g_last = gc[CHUNK - 1:CHUNK]
            xs.append(x)
            tinvs.append(eye + x)
            rest.append((jnp.concatenate([v * beta, kb * egc], axis=1), q * egc,
                         prods[CHUNK:] * decay, k * jnp.exp(g_last - gc), jnp.exp(g_last)))
        loc["rest"] = rest
        loc["xps"] = [_split(jnp.concatenate([xs[c] for c in grp], axis=1)) for grp in groups]
        loc["tps"] = [jnp.concatenate([tinvs[c] for c in grp], axis=1) for grp in groups]
        loc["diags"] = [(blockdiag(s[0]), blockdiag(s[1])) for s in loc["xps"]]

    def local_square():
        loc["xps"] = [_split(packed_dot3(s, d)) for s, d in zip(loc["xps"], loc["diags"])]
        loc["diags"] = [(blockdiag(s[0]), blockdiag(s[1])) for s in loc["xps"]]

    def local_accumulate():
        loc["tps"] = [t + packed_dot3(_split(t), d) for t, d in zip(loc["tps"], loc["diags"])]

    loc["uws"] = {}

    def local_solve(g):
        for j in range(GDN_PACK):
            c = groups[g][j]
            tinv = loc["tps"][g][:, j * CHUNK:(j + 1) * CHUNK]
            loc["uws"][c] = _dot3(_split(tinv), _split(loc["rest"][c][0]))

    def local_store():
        for c, r in enumerate(loc["rest"]):
            uw_st[c] = loc["uws"][c]
            qd_st[c] = r[1]
            qk_st[c] = r[2]
            kd_st[c] = r[3]
            gt_st[c] = jnp.broadcast_to(r[4], gt_st.shape[1:])

    levels = 0
    span = 1
    while 2 * span < CHUNK:
        levels += 1
        span *= 2
    local_stages = ([local_start] + [local_square, local_accumulate] * levels
                    + [functools.partial(local_solve, g) for g in range(len(groups))])
    rec_stages = [functools.partial(f, sub) for sub in range(nsub)
                  for f in (rec_read_state, rec_update)]
    for i in range(max(len(local_stages), len(rec_stages))):
        if i < len(local_stages):
            local_stages[i]()
        if i < len(rec_stages):
            rec_stages[i]()
    local_store()
    for b in range(nb):
        state_ref[b] = states[b]


def _gdn(qn, kn, vv, gcol, rows):
    batch, seq, _ = qn.shape
    nsteps = seq // rows
    nchains = batch * (rows // CHUNK)

    def cur(h, c):
        return (0, jnp.minimum(c, nsteps - 1), h)

    blk = pl.BlockSpec((batch, rows, LANES), cur)
    return pl.pallas_call(
        _gdn_kernel,
        grid=(GDN_HEADS, nsteps + 1),
        in_specs=[blk, blk, blk,
                  pl.BlockSpec((batch, rows, LANES), lambda h, c: (0, jnp.minimum(c, nsteps - 1), 0))],
        out_specs=pl.BlockSpec((batch, rows, LANES), lambda h, c: (0, jnp.maximum(c - 1, 0), h)),
        out_shape=jax.ShapeDtypeStruct(qn.shape, F32),
        scratch_shapes=[pltpu.VMEM((batch, GDN_DK, LANES), F32),
                        pltpu.VMEM((nchains, CHUNK, 2 * LANES), F32),
                        pltpu.VMEM((nchains, CHUNK, LANES), F32),
                        pltpu.VMEM((nchains, CHUNK, CHUNK), F32),
                        pltpu.VMEM((nchains, CHUNK, LANES), F32),
                        pltpu.VMEM((nchains, SUBLANES, LANES), F32)],
        compiler_params=_params(("parallel", "arbitrary")),
        name="gdn_chunks",
    )(qn, kn, vv, gcol)


def _merge_kernel(oa_ref, ob_ref, za_ref, zb_ref, ga_ref, gb_ref, x_ref, gng_ref, w_ref, fg_ref,
                  o_ref, mix_ref, *, final_norm):
    za = za_ref[...].astype(F32)
    ya = oa_ref[...] * (za * jax.nn.sigmoid(za))
    mix_ref[...] = (jax.nn.sigmoid(ga_ref[...].astype(F32)) * ya).astype(mix_ref.dtype)
    gng = gng_ref[...]
    for h in range(GDN_HEADS):
        sl = slice(h * LANES, (h + 1) * LANES)
        ob = ob_ref[:, sl]
        zb = zb_ref[:, sl].astype(F32)
        ms = jnp.mean(ob * ob, axis=-1, keepdims=True)
        yb = ob * lax.rsqrt(ms + NORM_EPS) * gng * (zb * jax.nn.sigmoid(zb))
        mix_ref[:, sl] += jax.nn.sigmoid(gb_ref[:, sl].astype(F32)) * yb
    y = x_ref[...] + jnp.dot(mix_ref[...].astype(MXU_DTYPE), w_ref[...], preferred_element_type=F32)
    if final_norm:
        ms = jnp.mean(y * y, axis=-1, keepdims=True)
        y = y * lax.rsqrt(ms + NORM_EPS) * fg_ref[...]
    o_ref[...] = y


def _merge_out(oa, ob, p, x2d, gdn_gain, w_out, final_gain, tm, final_norm):
    m = x2d.shape[0]

    def col(off):
        blk = off // D_MODEL
        return pl.BlockSpec((tm, D_MODEL), lambda i: (i, blk))

    row = pl.BlockSpec((tm, D_MODEL), lambda i: (i, 0))
    kern = functools.partial(_merge_kernel, final_norm=final_norm)
    return pl.pallas_call(
        kern,
        grid=(m // tm,),
        in_specs=[row, row, col(COL_ZA), col(COL_ZB), col(COL_GA), col(COL_GB), row,
                  pl.BlockSpec((1, LANES), lambda i: (0, 0)),
                  pl.BlockSpec((D_MODEL, D_MODEL), lambda i: (0, 0)),
                  pl.BlockSpec((1, D_MODEL), lambda i: (0, 0))],
        out_specs=row,
        out_shape=jax.ShapeDtypeStruct((m, D_MODEL), F32),
        scratch_shapes=[pltpu.VMEM((tm, D_MODEL), F32)],
        compiler_params=_params(("parallel",)),
        name="merge_out_proj",
    )(oa, ob, p, p, p, p, x2d, gdn_gain, w_out, final_gain)


def _pack_w_in(w):
    edges = [0]
    for n in IN_SPLITS:
        edges.append(edges[-1] + n)
    (qa, ka, va, za, qi, ki, wi, qkvb, zb, beta, a, gates) = [
        w[:, edges[j]:edges[j + 1]] for j in range(len(IN_SPLITS))]
    d = w.shape[0]
    qi = jnp.pad(qi.reshape(d, IDX_HEADS, IDX_DIM), ((0, 0), (0, 0), (0, LANES - IDX_DIM)))
    qi = qi.reshape(d, IDX_HEADS * LANES)
    ki = jnp.pad(ki, ((0, 0), (0, LANES - IDX_DIM)))
    misc = jnp.pad(jnp.concatenate([wi, beta, a], axis=1), ((0, 0), (0, LANES - 3 * 8)))
    packed = jnp.concatenate(
        [qa, za, zb, gates[:, :D_MODEL], gates[:, D_MODEL:], qkvb, qi, ka, va, ki, misc], axis=1)
    return packed.astype(MXU_DTYPE)


def _lane_pad(v, off):
    return jnp.pad(v.astype(F32), (off, LANES - off - v.shape[0])).reshape(1, LANES)


def _pick(n, prefs):
    for t in prefs:
        if n % t == 0:
            return t
    raise ValueError(f"no tile for {n}")


def kernel(x, positions, norm_gain, w_in, conv_w, a_log, dt_bias, gdn_norm_gain, idx_k_gain, w_out,
           final_gain):
    batch, seq, d = x.shape
    assert d == D_MODEL and seq % 128 == 0
    depth = w_in.shape[0]
    m = batch * seq
    tm_proj = _pick(m, (2048, 1024, 512, 256, 128))
    tn_proj = 768
    tm_prep = _pick(seq, (512, 256, 128))
    tm_merge = _pick(m, (256, 128))
    tq = 128
    tk = _pick(seq, (512, 256, 128))

    lane = jnp.arange(LANES)

    def inv_freq(half):
        rot = 2 * half
        f = ROPE_THETA ** (-jnp.arange(0, rot, 2, dtype=F32) / rot)
        return jnp.where(lane < rot, f[lane % half], 0.0).reshape(1, LANES).astype(F32)

    tabs = _rope_tables(positions.reshape(m, 1).astype(I32), inv_freq(ATTN_ROT_HALF),
                        inv_freq(IDX_ROT_HALF), tm_prep)
    tie_u = (jnp.arange(tk)[:, None] <= jnp.arange(tk)[None, :]).astype(BF16)

    x2d = x.reshape(m, d)
    for layer in range(depth):
        p_main, p_tail = _project(x2d, norm_gain[layer].reshape(1, d), _pack_w_in(w_in[layer]),
                                  tm_proj, tn_proj)

        kgain = _lane_pad(idx_k_gain[layer], 0)
        q, k, v, qi, ki, wi = _attn_prep(p_main, p_tail, tabs, kgain, tm_prep)
        shp = lambda t: t.reshape(batch, seq, t.shape[-1])
        o_a = _dsa(shp(q), shp(k), shp(v), shp(qi), shp(ki), shp(wi), tie_u, tq, tk).reshape(m, d)

        qn, kn, vv, gcol = _gdn_prep(p_main, p_tail, conv_w[layer], _lane_pad(a_log[layer], MISC_A),
                                     _lane_pad(dt_bias[layer], MISC_A), tm_prep, seq)
        o_b = _gdn(shp(qn), shp(kn), shp(vv), shp(gcol), 8 * CHUNK).reshape(m, d)

        x2d = _merge_out(o_a, o_b, p_main, x2d, gdn_norm_gain[layer].reshape(1, LANES),
                         w_out[layer].astype(MXU_DTYPE), final_gain.reshape(1, d), tm_merge,
                         final_norm=(layer == depth - 1))
    return x2d.reshape(batch, seq, d)
```

```python
import functools
import math

import jax
import jax.numpy as jnp
from jax import lax
from jax.experimental import pallas as pl
from jax.experimental.pallas import tpu as pltpu

F32 = jnp.float32
BF16 = jnp.bfloat16
I32 = jnp.int32

D_MODEL = 1024
ATTN_HEADS = 8
ATTN_KV_HEADS = 2
HEAD_DIM = 128
ATTN_GROUP = ATTN_HEADS // ATTN_KV_HEADS
ATTN_ROT_HALF = 16
IDX_HEADS = 8
IDX_DIM = 64
IDX_ROT_HALF = 8
INDEX_TOPK = 256
ROPE_THETA = 500000.0
GDN_HEADS = 8
GDN_DK = 128
CONV_K = 4
CHUNK = 64
NORM_EPS = 1e-6
IN_SPLITS = (1024, 256, 256, 1024, 512, 64, 8, 3072, 1024, 8, 8, 2048)

LANES = 128
SUBLANES = 8
VMEM_LIMIT = 56 * 1024 * 1024

COL_QA, COL_ZA, COL_ZB, COL_GA, COL_GB = 0, 1024, 2048, 3072, 4096
COL_QB, COL_KB, COL_VB, COL_QI = 5120, 6144, 7168, 8192
COL_KA, COL_VA, COL_KI, COL_MISC = 9216, 9472, 9728, 9856
P_COLS = 9984
MISC_WI, MISC_BETA, MISC_A = 0, 8, 16

MXU_DTYPE = BF16
GDN_PRECISION = lax.Precision.HIGHEST

KEY_MIN = -2 ** 31
NEG_BIG = -1e30
QK_SCALE_LOG2 = HEAD_DIM ** -0.5 * math.log2(math.e)
COUNT_ROWS = 128
UNCHECKED_BITS = 16
BITS_PER_CHECK = 2
GDN_PACK = 2


def _params(sem):
    return pltpu.CompilerParams(dimension_semantics=sem, vmem_limit_bytes=VMEM_LIMIT)


def _proj_kernel(x_ref, gain_ref, w_ref, o_ref, h_ref):
    @pl.when(pl.program_id(1) == 0)
    def _():
        x = x_ref[...]
        ms = jnp.mean(x * x, axis=-1, keepdims=True)
        h_ref[...] = (x * lax.rsqrt(ms + NORM_EPS) * gain_ref[...]).astype(h_ref.dtype)

    o_ref[...] = jnp.dot(h_ref[...], w_ref[...], preferred_element_type=F32)


def _project(x2d, gain, w_packed, tm, tn):
    m = x2d.shape[0]
    return pl.pallas_call(
        _proj_kernel,
        grid=(m // tm, P_COLS // tn),
        in_specs=[
            pl.BlockSpec((tm, D_MODEL), lambda i, j: (i, 0)),
            pl.BlockSpec((1, D_MODEL), lambda i, j: (0, 0)),
            pl.BlockSpec((D_MODEL, tn), lambda i, j: (0, j)),
        ],
        out_specs=pl.BlockSpec((tm, tn), lambda i, j: (i, j)),
        out_shape=jax.ShapeDtypeStruct((m, P_COLS), F32),
        scratch_shapes=[pltpu.VMEM((tm, D_MODEL), MXU_DTYPE)],
        compiler_params=_params(("parallel", "arbitrary")),
        name="in_proj",
    )(x2d, gain, w_packed)


def _rope_tab_kernel(pos_ref, inva_ref, invi_ref, cosa_ref, sina_ref, cosi_ref, sini_ref):
    pos = pos_ref[...].astype(F32)
    lane = lax.broadcasted_iota(I32, (1, LANES), 1)

    def tables(inv, half, cos_ref, sin_ref):
        ang = pos * inv
        c = jnp.cos(ang)
        s = jnp.sin(ang)
        cos_ref[...] = jnp.where(lane < 2 * half, c, 1.0)
        sin_ref[...] = jnp.where(lane < half, -s, jnp.where(lane < 2 * half, s, 0.0))

    tables(inva_ref[...], ATTN_ROT_HALF, cosa_ref, sina_ref)
    tables(invi_ref[...], IDX_ROT_HALF, cosi_ref, sini_ref)


def _rope_tables(pos2d, inv_a, inv_i, tm):
    m = pos2d.shape[0]
    tab = jax.ShapeDtypeStruct((m, LANES), F32)
    row = pl.BlockSpec((tm, LANES), lambda i: (i, 0))
    one = pl.BlockSpec((1, LANES), lambda i: (0, 0))
    return pl.pallas_call(
        _rope_tab_kernel,
        grid=(m // tm,),
        in_specs=[pl.BlockSpec((tm, 1), lambda i: (i, 0)), one, one],
        out_specs=[row, row, row, row],
        out_shape=[tab, tab, tab, tab],
        compiler_params=_params(("parallel",)),
        name="rope_tables",
    )(pos2d, inv_a, inv_i)


def _rope(x, c, s, half):
    lane = lax.broadcasted_iota(I32, x.shape, 1)
    partner = jnp.where(lane < half, pltpu.roll(x, LANES - half, 1), pltpu.roll(x, half, 1))
    return x * c + partner * s


def _attn_prep_kernel(qa_ref, ka_ref, va_ref, qi_ref, ki_ref, misc_ref,
                      cosa_ref, sina_ref, cosi_ref, sini_ref, kgain_ref,
                      q_out, k_out, v_out, qi_out, ki_out, wi_out):
    ca, sa = cosa_ref[...], sina_ref[...]
    ci, si = cosi_ref[...], sini_ref[...]
    for h in range(ATTN_HEADS):
        sl = slice(h * LANES, (h + 1) * LANES)
        q_out[:, sl] = (_rope(qa_ref[:, sl], ca, sa, ATTN_ROT_HALF) * QK_SCALE_LOG2).astype(q_out.dtype)
    ones = jnp.ones((va_ref.shape[0], LANES), v_out.dtype)
    for h in range(ATTN_KV_HEADS):
        sl = slice(h * LANES, (h + 1) * LANES)
        k_out[:, sl] = _rope(ka_ref[:, sl], ca, sa, ATTN_ROT_HALF).astype(k_out.dtype)
        v_out[:, 2 * h * LANES:(2 * h + 1) * LANES] = va_ref[:, sl].astype(v_out.dtype)
        v_out[:, (2 * h + 1) * LANES:(2 * h + 2) * LANES] = ones
    for h in range(IDX_HEADS):
        sl = slice(h * LANES, (h + 1) * LANES)
        qi_out[:, sl] = _rope(qi_ref[:, sl], ci, si, IDX_ROT_HALF).astype(qi_out.dtype)
    ki = ki_ref[...]
    ms = jnp.sum(ki * ki, axis=-1, keepdims=True) * (1.0 / IDX_DIM)
    ki = ki * lax.rsqrt(ms + NORM_EPS) * kgain_ref[...]
    ki_out[...] = _rope(ki, ci, si, IDX_ROT_HALF).astype(ki_out.dtype)
    wi_out[...] = misc_ref[...] * (IDX_HEADS ** -0.5 * IDX_DIM ** -0.5)


def _attn_prep(p, tabs, kgain, tm):
    m = p.shape[0]
    cosa, sina, cosi, sini = tabs

    def col(width, off):
        blk = off // width
        return pl.BlockSpec((tm, width), lambda i: (i, blk))

    row = pl.BlockSpec((tm, LANES), lambda i: (i, 0))
    outs = [(ATTN_HEADS * LANES, MXU_DTYPE), (ATTN_KV_HEADS * LANES, MXU_DTYPE),
            (2 * ATTN_KV_HEADS * LANES, MXU_DTYPE), (IDX_HEADS * LANES, MXU_DTYPE),
            (LANES, MXU_DTYPE), (LANES, F32)]
    return pl.pallas_call(
        _attn_prep_kernel,
        grid=(m // tm,),
        in_specs=[col(1024, COL_QA), col(256, COL_KA), col(256, COL_VA), col(1024, COL_QI),
                  col(128, COL_KI), col(128, COL_MISC), row, row, row, row,
                  pl.BlockSpec((1, LANES), lambda i: (0, 0))],
        out_specs=[pl.BlockSpec((tm, w), lambda i: (i, 0)) for w, _ in outs],
        out_shape=[jax.ShapeDtypeStruct((m, w), d) for w, d in outs],
        compiler_params=_params(("parallel",)),
        name="attn_prep",
    )(p, p, p, p, p, p, cosa, sina, cosi, sini, kgain)


def _nt_dot(a, b):
    return lax.dot_general(a, b, (((1,), (1,)), ((), ())), preferred_element_type=F32)


def _dsa_kernel(q_ref, qi_ref, wi_ref, k_ref, v_ref, kidx_ref, u_ref, o_ref,
                keys_ref, m_ref, acc_ref, *, tq, tk, topk):
    i = pl.program_id(1)
    nkt = (i * tq + tq + tk - 1) // tk
    row_t = i * tq + lax.broadcasted_iota(I32, (tq, 1), 0)

    qi = qi_ref[...]
    wi = wi_ref[...]

    wb = [jnp.tile(jnp.broadcast_to(wi[:, MISC_WI + h:MISC_WI + h + 1], (tq, LANES)),
                   (1, tk // LANES)) for h in range(IDX_HEADS)]

    def index_tiles(kts):
        offs = [pl.multiple_of(kt * tk, tk) for kt in kts]
        raws = [[_nt_dot(qi[:, h * LANES:(h + 1) * LANES], kidx_ref[pl.ds(off, tk), :])
                 for h in range(IDX_HEADS)] for off in offs]
        for kt, off, raw in zip(kts, offs, raws):
            acc = jnp.maximum(raw[0], 0.0) * wb[0]
            for h in range(1, IDX_HEADS):
                acc = acc + jnp.maximum(raw[h], 0.0) * wb[h]
            bits = pltpu.bitcast(acc, I32)
            key = jnp.where(bits < 0, (bits ^ jnp.int32(0x7FFFFFFF)) + 1, bits)
            col = off + lax.broadcasted_iota(I32, (1, tk), 1)
            keys_ref[kt] = jnp.where(col <= row_t, key, jnp.int32(KEY_MIN))

    def idx_pair(j, carry):
        index_tiles([2 * j, 2 * j + 1])
        return carry

    lax.fori_loop(0, nkt // 2, idx_pair, 0)

    @pl.when(nkt % 2 == 1)
    def _():
        index_tiles([nkt - 1])

    def count_ge(cand):
        parts = []
        for r0 in range(0, tq, COUNT_ROWS):
            candb = jnp.broadcast_to(cand[r0:r0 + COUNT_ROWS], (COUNT_ROWS, LANES))

            def one(kt, acc, r0=r0, candb=candb):
                for c in range(tk // LANES):
                    kk = keys_ref[kt, r0:r0 + COUNT_ROWS, c * LANES:(c + 1) * LANES]
                    acc = acc + jnp.where(kk >= candb, 1.0, 0.0)
                return acc

            def two(j, acc, one=one):
                return one(2 * j + 1, one(2 * j, acc))

            acc = lax.fori_loop(0, nkt // 2, two, jnp.zeros((COUNT_ROWS, LANES), F32))
            acc = lax.cond(nkt % 2 == 1, functools.partial(one, nkt - 1), lambda a: a, acc)
            parts.append(jnp.sum(acc, axis=1, keepdims=True))
        return jnp.concatenate(parts, axis=0)

    kf = float(topk)

    def bit_step(b, ans, cnt):
        cand = ans + lax.shift_left(jnp.int32(1), 31 - b)
        cc = count_ge(cand)
        take = cc >= kf
        return jnp.where(take, cand, ans), jnp.where(take, cc, cnt)

    def bits_cond(c):
        b, _, cnt = c
        return (b < 32) & (jnp.max(jnp.abs(cnt - kf)) > 0.0)

    def bits_body(c):
        b, ans, cnt = c
        for u in range(BITS_PER_CHECK):
            ans, cnt = bit_step(b + u, ans, cnt)
        return b + BITS_PER_CHECK, ans, cnt

    thr, cnt = lax.fori_loop(
        0, UNCHECKED_BITS, lambda b, c: bit_step(b, *c),
        (jnp.full((tq, 1), KEY_MIN, I32), jnp.full((tq, 1), nkt * tk, I32).astype(F32)))
    _, thr, cnt = lax.while_loop(bits_cond, bits_body, (jnp.int32(UNCHECKED_BITS), thr, cnt))
    short = thr == jnp.int32(KEY_MIN)
    has_tie = jnp.max(jnp.where(short, 0.0, cnt - kf)) > 0.0

    q = q_ref[...]
    qg = [jnp.concatenate([q[:, (g * ATTN_GROUP + hh) * LANES:(g * ATTN_GROUP + hh + 1) * LANES]
                           for hh in range(ATTN_GROUP)], axis=0) for g in range(ATTN_KV_HEADS)]
    m_ref[...] = jnp.full(m_ref.shape, NEG_BIG, F32)
    acc_ref[...] = jnp.zeros(acc_ref.shape, F32)
    lane_tiles = tk // LANES

    def attend(tiles):
        units = [(pl.multiple_of(kt * tk, tk), bias, g) for kt, bias in tiles
                 for g in range(ATTN_KV_HEADS)]

        def qk(unit):
            off, _, g = unit
            return _nt_dot(qg[g], k_ref[pl.ds(off, tk), g * LANES:(g + 1) * LANES])

        ahead = 2
        scores = [qk(u) for u in units[:ahead]]
        for i, (off, bias, g) in enumerate(units):
            ps, alphas = [], []
            for hh in range(ATTN_GROUP):
                h = g * ATTN_GROUP + hh
                s = scores[i][hh * tq:(hh + 1) * tq] + bias
                m_prev = m_ref[h]
                m_new = jnp.maximum(m_prev, jnp.max(s, axis=1, keepdims=True))
                m_ref[h] = m_new
                alphas.append(jnp.exp2(m_prev - m_new))
                ps.append(jnp.exp2(s - jnp.tile(m_new, (1, lane_tiles))).astype(MXU_DTYPE))
            if i + ahead < len(units):
                scores.append(qk(units[i + ahead]))
            v_t = v_ref[pl.ds(off, tk), 2 * g * LANES:(2 * g + 2) * LANES]
            pv = jnp.dot(jnp.concatenate(ps, axis=0), v_t, preferred_element_type=F32)
            for hh in range(ATTN_GROUP):
                h = g * ATTN_GROUP + hh
                acc_ref[h] = (jnp.tile(alphas[hh], (1, 2)) * acc_ref[h]
                              + pv[hh * tq:(hh + 1) * tq])

    thr_floor = jnp.maximum(thr, jnp.int32(KEY_MIN + 1))

    def plain_bias(kt):
        return jnp.where(keys_ref[kt] >= thr_floor, 0.0, NEG_BIG)

    def pair_body(j, carry):
        attend([(2 * j, plain_bias(2 * j)), (2 * j + 1, plain_bias(2 * j + 1))])
        return carry

    def run_plain():
        lax.fori_loop(0, nkt // 2, pair_body, 0)

        @pl.when(nkt % 2 == 1)
        def _():
            attend([(nkt - 1, plain_bias(nkt - 1))])

    def run_ties():
        need = jnp.where(short, 0.0, kf - count_ge(thr + 1))

        def tie_body(kt, tie_carry):
            kk = keys_ref[kt]
            eq = kk == thr
            prefix = jnp.dot(jnp.where(eq, 1.0, 0.0).astype(BF16), u_ref[...],
                             preferred_element_type=F32)
            rank_ok = (prefix + tie_carry) <= need
            attend([(kt, jnp.where(kk > thr, 0.0,
                                   jnp.where(eq, jnp.where(rank_ok, 0.0, NEG_BIG), NEG_BIG)))])
            return tie_carry + prefix[:, tk - 1:tk]

        lax.fori_loop(0, nkt, tie_body, jnp.zeros((tq, 1), F32))

    lax.cond(has_tie, run_ties, run_plain)

    for h in range(ATTN_HEADS):
        acc = acc_ref[h]
        o_ref[:, h * LANES:(h + 1) * LANES] = (acc[:, :LANES] / acc[:, LANES:]).astype(o_ref.dtype)


def _dsa(q, k, v, qi, ki, wi, tie_u, tq, tk):
    b, s, _ = q.shape
    topk = min(INDEX_TOPK, s // 4)
    kern = functools.partial(_dsa_kernel, tq=tq, tk=tk, topk=topk)

    def qblk(w):
        return pl.BlockSpec((None, tq, w), lambda bi, i: (bi, i, 0))

    def full(w):
        return pl.BlockSpec((None, s, w), lambda bi, i: (bi, 0, 0))

    return pl.pallas_call(
        kern,
        grid=(b, s // tq),
        in_specs=[qblk(ATTN_HEADS * LANES), qblk(IDX_HEADS * LANES), qblk(LANES),
                  full(ATTN_KV_HEADS * LANES), full(2 * ATTN_KV_HEADS * LANES), full(LANES),
                  pl.BlockSpec((tk, tk), lambda bi, i: (0, 0))],
        out_specs=qblk(ATTN_HEADS * LANES),
        out_shape=jax.ShapeDtypeStruct((b, s, ATTN_HEADS * LANES), F32),
        scratch_shapes=[pltpu.VMEM((s // tk, tq, tk), I32),
                        pltpu.VMEM((ATTN_HEADS, tq, LANES), F32),
                        pltpu.VMEM((ATTN_HEADS, tq, 2 * LANES), F32)],
        compiler_params=_params(("parallel", "arbitrary")),
        name="dsa_attention",
    )(q, qi, wi, k, v, ki, tie_u)


def _gdn_prep_kernel(q_ref, k_ref, v_ref, qh_ref, kh_ref, vh_ref, misc_ref,
                     wq_ref, wk_ref, wv_ref, alog_ref, dtb_ref,
                     qn_out, kn_out, vv_out, gcol_out, buf_ref, *, tm, seq):
    seq_start = (pl.program_id(0) * tm) % seq == 0

    def conv_silu(x_ref, halo_ref, w_ref):
        x = x_ref[...]
        buf_ref[:SUBLANES, :] = jnp.where(seq_start, 0.0, halo_ref[...])
        buf_ref[SUBLANES:, :] = x
        w = w_ref[...]
        y = x * w[CONV_K - 1:CONV_K]
        for d in range(1, CONV_K):
            y = y + buf_ref[SUBLANES - d:SUBLANES - d + tm, :] * w[CONV_K - 1 - d:CONV_K - d]
        return y * jax.nn.sigmoid(y)

    def l2norm(x, out_ref, mult):
        for h in range(GDN_HEADS):
            sl = slice(h * LANES, (h + 1) * LANES)
            xh = x[:, sl]
            ss = jnp.sum(xh * xh, axis=-1, keepdims=True)
            out_ref[:, sl] = xh * (lax.rsqrt(ss + NORM_EPS) * mult)

    l2norm(conv_silu(q_ref, qh_ref, wq_ref), qn_out, GDN_DK ** -0.5)
    l2norm(conv_silu(k_ref, kh_ref, wk_ref), kn_out, 1.0)
    vv_out[...] = conv_silu(v_ref, vh_ref, wv_ref)

    misc = misc_ref[...]
    lane = lax.broadcasted_iota(I32, misc.shape, 1)
    row = lax.broadcasted_iota(I32, misc.shape, 0)
    is_a = (lane >= MISC_A) & (lane < MISC_A + GDN_HEADS)
    z = misc + dtb_ref[...]
    softplus = jnp.maximum(z, 0.0) + jnp.log(1.0 + jnp.exp(-jnp.abs(z)))
    g = jnp.where(is_a, -jnp.exp(alog_ref[...]) * softplus, 0.0)
    sh = 1
    while sh < CHUNK:
        g = g + jnp.where((row % CHUNK) >= sh, pltpu.roll(g, sh, 0), 0.0)
        sh *= 2
    is_beta = (lane >= MISC_BETA) & (lane < MISC_BETA + GDN_HEADS)
    gcol_out[...] = jnp.where(is_beta, jax.nn.sigmoid(misc), g)


def _gdn_prep(p, conv_w, alog_pad, dtb_pad, tm, seq):
    m = p.shape[0]
    wide = GDN_HEADS * LANES
    halo_blocks = tm // SUBLANES

    def col(off):
        blk = off // wide
        return pl.BlockSpec((tm, wide), lambda i: (i, blk))

    def halo(off):
        blk = off // wide
        return pl.BlockSpec((SUBLANES, wide), lambda i: (jnp.maximum(i * halo_blocks - 1, 0), blk))

    def wblk(j):
        return pl.BlockSpec((CONV_K, wide), lambda i: (0, j))

    one = pl.BlockSpec((1, LANES), lambda i: (0, 0))
    big = jax.ShapeDtypeStruct((m, wide), F32)
    kern = functools.partial(_gdn_prep_kernel, tm=tm, seq=seq)
    return pl.pallas_call(
        kern,
        grid=(m // tm,),
        in_specs=[col(COL_QB), col(COL_KB), col(COL_VB), halo(COL_QB), halo(COL_KB), halo(COL_VB),
                  pl.BlockSpec((tm, LANES), lambda i: (i, COL_MISC // LANES)),
                  wblk(0), wblk(1), wblk(2), one, one],
        out_specs=[pl.BlockSpec((tm, wide), lambda i: (i, 0))] * 3
        + [pl.BlockSpec((tm, LANES), lambda i: (i, 0))],
        out_shape=[big, big, big, jax.ShapeDtypeStruct((m, LANES), F32)],
        scratch_shapes=[pltpu.VMEM((tm + SUBLANES, wide), F32)],
        compiler_params=_params(("parallel",)),
        name="gdn_prep",
    )(p, p, p, p, p, p, p, conv_w, conv_w, conv_w, alog_pad, dtb_pad)


def _split(a):
    hi = a.astype(BF16)
    return hi, (a - hi.astype(F32)).astype(BF16)


def _dot3(a_split, b_split):
    (ah, al), (bh, bl) = a_split, b_split
    m = ah.shape[0]
    both = jnp.dot(jnp.concatenate([ah, al], axis=0), bh, preferred_element_type=F32)
    return both[:m] + both[m:] + jnp.dot(ah, bl, preferred_element_type=F32)


def _mxu_dot(a, b, dims=(((1,), (0,)), ((), ()))):
    return lax.dot_general(a.astype(MXU_DTYPE), b.astype(MXU_DTYPE), dims,
                           preferred_element_type=F32)


def _gdn_kernel(q_ref, k_ref, v_ref, gcol_ref, o_ref,
                state_ref, uw_st, qd_st, qk_st, kd_st, gt_st):
    h = pl.program_id(0)

    @pl.when(pl.program_id(1) == 0)
    def _():
        for ref in (state_ref, uw_st, qd_st, qk_st, kd_st, gt_st):
            ref[...] = jnp.zeros(ref.shape, F32)

    nb, rows = q_ref.shape[0], q_ref.shape[1]
    ii = lax.broadcasted_iota(I32, (CHUNK, CHUNK), 0)
    jj = lax.broadcasted_iota(I32, (CHUNK, CHUNK), 1)
    incl = ii >= jj
    strict = ii > jj
    eye = jnp.where(ii == jj, 1.0, 0.0)
    nt = (((1,), (1,)), ((), ()))
    tn = (((0,), (0,)), ((), ()))
    nsub = rows // CHUNK
    chains = [(b, sub) for b in range(nb) for sub in range(nsub)]

    states = [state_ref[b] for b in range(nb)]

    rec = {}

    def rec_read_state(sub):
        idx = [chains.index((b, sub)) for b in range(nb)]
        uws = [uw_st[c] for c in idx]
        rec["wss"] = [_mxu_dot(jnp.concatenate([uw[:, LANES:], qd_st[c]], axis=0), states[b])
                      for b, (c, uw) in enumerate(zip(idx, uws))]
        rec["v_news"] = [uw[:, :LANES] - ws[:CHUNK] for uw, ws in zip(uws, rec["wss"])]

    def rec_update(sub):
        sl = slice(sub * CHUNK, (sub + 1) * CHUNK)
        idx = [chains.index((b, sub)) for b in range(nb)]
        for b, c in enumerate(idx):
            o_ref[b, sl, :] = rec["wss"][b][CHUNK:] + _mxu_dot(qk_st[c], rec["v_news"][b])
        for b, c in enumerate(idx):
            states[b] = states[b] * gt_st[c][:1] + _mxu_dot(kd_st[c], rec["v_news"][b], tn)

    wide = GDN_PACK * CHUNK
    same_block = (lax.broadcasted_iota(I32, (wide, wide), 0) // CHUNK
                  == lax.broadcasted_iota(I32, (wide, wide), 1) // CHUNK)
    groups = [list(range(g, g + GDN_PACK)) for g in range(0, len(chains), GDN_PACK)]
    loc = {}

    def blockdiag(m):
        return jnp.where(same_block, jnp.tile(m, (GDN_PACK, 1)), jnp.zeros((), m.dtype))

    def packed_dot3(a_split, b_diag):
        (ah, al), (bh, bl) = a_split, b_diag
        both = jnp.dot(jnp.concatenate([ah, al], axis=0), bh, preferred_element_type=F32)
        return both[:CHUNK] + both[CHUNK:] + jnp.dot(ah, bl, preferred_element_type=F32)

    def local_start():
        col = {}
        for b in range(nb):
            gcol = gcol_ref[b]
            lane = lax.broadcasted_iota(I32, gcol.shape, 1)
            beta_all = jnp.sum(jnp.where(lane == MISC_BETA + h, gcol, 0.0), axis=1, keepdims=True)
            gc_all = jnp.sum(jnp.where(lane == MISC_A + h, gcol, 0.0), axis=1, keepdims=True)
            gc_sq = jnp.broadcast_to(gc_all, (rows, rows))
            col[b] = (beta_all, gc_all, gc_sq, gc_sq.T)
        xs, tinvs, rest = [], [], []
        for b, sub in chains:
            sl = slice(sub * CHUNK, (sub + 1) * CHUNK)
            beta_all, gc_all, gc_sq, gcr_sq = col[b]
            q, k, v = q_ref[b, sl, :], k_ref[b, sl, :], v_ref[b, sl, :]
            beta, gc = beta_all[sl], gc_all[sl]
            decay = jnp.where(incl, jnp.exp(gc_sq[sl, sl] - gcr_sq[sl, sl]), 0.0)
            kb = k * beta
            prods = _mxu_dot(jnp.concatenate([kb, q], axis=0), k, nt)
            x = -jnp.where(strict, prods[:CHUNK] * decay, 0.0)
            egc = jnp.exp(gc)
            g_last = gc[CHUNK - 1:CHUNK]
            xs.append(x)
            tinvs.append(eye + x)
            rest.append((jnp.concatenate([v * beta, kb * egc], axis=1), q * egc,
                         prods[CHUNK:] * decay, k * jnp.exp(g_last - gc), jnp.exp(g_last)))
        loc["rest"] = rest
        loc["xps"] = [_split(jnp.concatenate([xs[c] for c in grp], axis=1)) for grp in groups]
        loc["tps"] = [jnp.concatenate([tinvs[c] for c in grp], axis=1) for grp in groups]
        loc["diags"] = [(blockdiag(s[0]), blockdiag(s[1])) for s in loc["xps"]]

    def local_square():
        loc["xps"] = [_split(packed_dot3(s, d)) for s, d in zip(loc["xps"], loc["diags"])]
        loc["diags"] = [(blockdiag(s[0]), blockdiag(s[1])) for s in loc["xps"]]

    def local_accumulate():
        loc["tps"] = [t + packed_dot3(_split(t), d) for t, d in zip(loc["tps"], loc["diags"])]

    loc["uws"] = {}

    def local_solve(g):
        for j in range(GDN_PACK):
            c = groups[g][j]
            tinv = loc["tps"][g][:, j * CHUNK:(j + 1) * CHUNK]
            loc["uws"][c] = _dot3(_split(tinv), _split(loc["rest"][c][0]))

    def local_store():
        for c, r in enumerate(loc["rest"]):
            uw_st[c] = loc["uws"][c]
            qd_st[c] = r[1]
            qk_st[c] = r[2]
            kd_st[c] = r[3]
            gt_st[c] = jnp.broadcast_to(r[4], gt_st.shape[1:])

    levels = 0
    span = 1
    while 2 * span < CHUNK:
        levels += 1
        span *= 2
    local_stages = ([local_start] + [local_square, local_accumulate] * levels
                    + [functools.partial(local_solve, g) for g in range(len(groups))])
    rec_stages = [functools.partial(f, sub) for sub in range(nsub)
                  for f in (rec_read_state, rec_update)]
    for i in range(max(len(local_stages), len(rec_stages))):
        if i < len(local_stages):
            local_stages[i]()
        if i < len(rec_stages):
            rec_stages[i]()
    local_store()
    for b in range(nb):
        state_ref[b] = states[b]


def _gdn(qn, kn, vv, gcol, rows):
    batch, seq, _ = qn.shape
    nsteps = seq // rows
    nchains = batch * (rows // CHUNK)

    def cur(h, c):
        return (0, jnp.minimum(c, nsteps - 1), h)

    blk = pl.BlockSpec((batch, rows, LANES), cur)
    return pl.pallas_call(
        _gdn_kernel,
        grid=(GDN_HEADS, nsteps + 1),
        in_specs=[blk, blk, blk,
                  pl.BlockSpec((batch, rows, LANES), lambda h, c: (0, jnp.minimum(c, nsteps - 1), 0))],
        out_specs=pl.BlockSpec((batch, rows, LANES), lambda h, c: (0, jnp.maximum(c - 1, 0), h)),
        out_shape=jax.ShapeDtypeStruct(qn.shape, F32),
        scratch_shapes=[pltpu.VMEM((batch, GDN_DK, LANES), F32),
                        pltpu.VMEM((nchains, CHUNK, 2 * LANES), F32),
                        pltpu.VMEM((nchains, CHUNK, LANES), F32),
                        pltpu.VMEM((nchains, CHUNK, CHUNK), F32),
                        pltpu.VMEM((nchains, CHUNK, LANES), F32),
                        pltpu.VMEM((nchains, SUBLANES, LANES), F32)],
        compiler_params=_params(("parallel", "arbitrary")),
        name="gdn_chunks",
    )(qn, kn, vv, gcol)


def _merge_kernel(oa_ref, ob_ref, za_ref, zb_ref, ga_ref, gb_ref, x_ref, gng_ref, w_ref, fg_ref,
                  o_ref, mix_ref, *, final_norm):
    za = za_ref[...]
    ya = oa_ref[...] * (za * jax.nn.sigmoid(za))
    mix_ref[...] = (jax.nn.sigmoid(ga_ref[...]) * ya).astype(mix_ref.dtype)
    gng = gng_ref[...]
    for h in range(GDN_HEADS):
        sl = slice(h * LANES, (h + 1) * LANES)
        ob = ob_ref[:, sl]
        zb = zb_ref[:, sl]
        ms = jnp.mean(ob * ob, axis=-1, keepdims=True)
        yb = ob * lax.rsqrt(ms + NORM_EPS) * gng * (zb * jax.nn.sigmoid(zb))
        mix_ref[:, sl] += jax.nn.sigmoid(gb_ref[:, sl]) * yb
    y = x_ref[...] + jnp.dot(mix_ref[...].astype(MXU_DTYPE), w_ref[...], preferred_element_type=F32)
    if final_norm:
        ms = jnp.mean(y * y, axis=-1, keepdims=True)
        y = y * lax.rsqrt(ms + NORM_EPS) * fg_ref[...]
    o_ref[...] = y


def _merge_out(oa, ob, p, x2d, gdn_gain, w_out, final_gain, tm, final_norm):
    m = x2d.shape[0]

    def col(off):
        blk = off // D_MODEL
        return pl.BlockSpec((tm, D_MODEL), lambda i: (i, blk))

    row = pl.BlockSpec((tm, D_MODEL), lambda i: (i, 0))
    kern = functools.partial(_merge_kernel, final_norm=final_norm)
    return pl.pallas_call(
        kern,
        grid=(m // tm,),
        in_specs=[row, row, col(COL_ZA), col(COL_ZB), col(COL_GA), col(COL_GB), row,
                  pl.BlockSpec((1, LANES), lambda i: (0, 0)),
                  pl.BlockSpec((D_MODEL, D_MODEL), lambda i: (0, 0)),
                  pl.BlockSpec((1, D_MODEL), lambda i: (0, 0))],
        out_specs=row,
        out_shape=jax.ShapeDtypeStruct((m, D_MODEL), F32),
        scratch_shapes=[pltpu.VMEM((tm, D_MODEL), F32)],
        compiler_params=_params(("parallel",)),
        name="merge_out_proj",
    )(oa, ob, p, p, p, p, x2d, gdn_gain, w_out, final_gain)


def _pack_w_in(w):
    edges = [0]
    for n in IN_SPLITS:
        edges.append(edges[-1] + n)
    (qa, ka, va, za, qi, ki, wi, qkvb, zb, beta, a, gates) = [
        w[:, edges[j]:edges[j + 1]] for j in range(len(IN_SPLITS))]
    d = w.shape[0]
    qi = jnp.pad(qi.reshape(d, IDX_HEADS, IDX_DIM), ((0, 0), (0, 0), (0, LANES - IDX_DIM)))
    qi = qi.reshape(d, IDX_HEADS * LANES)
    ki = jnp.pad(ki, ((0, 0), (0, LANES - IDX_DIM)))
    misc = jnp.pad(jnp.concatenate([wi, beta, a], axis=1), ((0, 0), (0, LANES - 3 * 8)))
    packed = jnp.concatenate(
        [qa, za, zb, gates[:, :D_MODEL], gates[:, D_MODEL:], qkvb, qi, ka, va, ki, misc], axis=1)
    return packed.astype(MXU_DTYPE)


def _lane_pad(v, off):
    return jnp.pad(v.astype(F32), (off, LANES - off - v.shape[0])).reshape(1, LANES)


def _pick(n, prefs):
    for t in prefs:
        if n % t == 0:
            return t
    raise ValueError(f"no tile for {n}")


def kernel(x, positions, norm_gain, w_in, conv_w, a_log, dt_bias, gdn_norm_gain, idx_k_gain, w_out,
           final_gain):
    batch, seq, d = x.shape
    assert d == D_MODEL and seq % 128 == 0
    depth = w_in.shape[0]
    m = batch * seq
    tm_proj = _pick(m, (2048, 1024, 512, 256, 128))
    tn_proj = 768
    tm_prep = _pick(seq, (512, 256, 128))
    tm_merge = _pick(m, (256, 128))
    tq = 128
    tk = _pick(seq, (512, 256, 128))

    lane = jnp.arange(LANES)

    def inv_freq(half):
        rot = 2 * half
        f = ROPE_THETA ** (-jnp.arange(0, rot, 2, dtype=F32) / rot)
        return jnp.where(lane < rot, f[lane % half], 0.0).reshape(1, LANES).astype(F32)

    tabs = _rope_tables(positions.reshape(m, 1).astype(I32), inv_freq(ATTN_ROT_HALF),
                        inv_freq(IDX_ROT_HALF), tm_prep)
    tie_u = (jnp.arange(tk)[:, None] <= jnp.arange(tk)[None, :]).astype(BF16)

    x2d = x.reshape(m, d)
    for layer in range(depth):
        p = _project(x2d, norm_gain[layer].reshape(1, d), _pack_w_in(w_in[layer]), tm_proj, tn_proj)

        kgain = _lane_pad(idx_k_gain[layer], 0)
        q, k, v, qi, ki, wi = _attn_prep(p, tabs, kgain, tm_prep)
        shp = lambda t: t.reshape(batch, seq, t.shape[-1])
        o_a = _dsa(shp(q), shp(k), shp(v), shp(qi), shp(ki), shp(wi), tie_u, tq, tk).reshape(m, d)

        qn, kn, vv, gcol = _gdn_prep(p, conv_w[layer], _lane_pad(a_log[layer], MISC_A),
                                     _lane_pad(dt_bias[layer], MISC_A), tm_prep, seq)
        o_b = _gdn(shp(qn), shp(kn), shp(vv), shp(gcol), 8 * CHUNK).reshape(m, d)

        x2d = _merge_out(o_a, o_b, p, x2d, gdn_norm_gain[layer].reshape(1, LANES),
                         w_out[layer].astype(MXU_DTYPE), final_gain.reshape(1, d), tm_merge,
                         final_norm=(layer == depth - 1))
    return x2d.reshape(batch, seq, d)
```

```python
import functools
import math

import jax
import jax.numpy as jnp
from jax import lax
from jax.experimental import pallas as pl
from jax.experimental.pallas import tpu as pltpu

F32 = jnp.float32
BF16 = jnp.bfloat16
I32 = jnp.int32

D_MODEL = 1024
ATTN_HEADS = 8
ATTN_KV_HEADS = 2
HEAD_DIM = 128
ATTN_GROUP = ATTN_HEADS // ATTN_KV_HEADS
ATTN_ROT_HALF = 16
IDX_HEADS = 8
IDX_DIM = 64
IDX_ROT_HALF = 8
INDEX_TOPK = 256
ROPE_THETA = 500000.0
GDN_HEADS = 8
GDN_DK = 128
CONV_K = 4
CHUNK = 64
NORM_EPS = 1e-6
IN_SPLITS = (1024, 256, 256, 1024, 512, 64, 8, 3072, 1024, 8, 8, 2048)

LANES = 128
SUBLANES = 8
VMEM_LIMIT = 56 * 1024 * 1024

COL_QA, COL_ZA, COL_ZB, COL_GA, COL_GB = 0, 1024, 2048, 3072, 4096
COL_QB, COL_KB, COL_VB, COL_QI = 5120, 6144, 7168, 8192
COL_KA, COL_VA, COL_KI, COL_MISC = 9216, 9472, 9728, 9856
P_COLS = 9984
MISC_WI, MISC_BETA, MISC_A = 0, 8, 16

MXU_DTYPE = BF16
GDN_PRECISION = lax.Precision.HIGHEST

KEY_MIN = -2 ** 31
NEG_BIG = -1e30
QK_SCALE_LOG2 = HEAD_DIM ** -0.5 * math.log2(math.e)
COUNT_ROWS = 128
UNCHECKED_BITS = 16
BITS_PER_CHECK = 2
UNIT_HEADS = 2
GDN_PACK = 2


def _params(sem):
    return pltpu.CompilerParams(dimension_semantics=sem, vmem_limit_bytes=VMEM_LIMIT)


def _proj_kernel(x_ref, gain_ref, w_ref, o_ref, h_ref):
    @pl.when(pl.program_id(1) == 0)
    def _():
        x = x_ref[...]
        ms = jnp.mean(x * x, axis=-1, keepdims=True)
        h_ref[...] = (x * lax.rsqrt(ms + NORM_EPS) * gain_ref[...]).astype(h_ref.dtype)

    o_ref[...] = jnp.dot(h_ref[...], w_ref[...], preferred_element_type=F32)


def _project(x2d, gain, w_packed, tm, tn):
    m = x2d.shape[0]
    return pl.pallas_call(
        _proj_kernel,
        grid=(m // tm, P_COLS // tn),
        in_specs=[
            pl.BlockSpec((tm, D_MODEL), lambda i, j: (i, 0)),
            pl.BlockSpec((1, D_MODEL), lambda i, j: (0, 0)),
            pl.BlockSpec((D_MODEL, tn), lambda i, j: (0, j)),
        ],
        out_specs=pl.BlockSpec((tm, tn), lambda i, j: (i, j)),
        out_shape=jax.ShapeDtypeStruct((m, P_COLS), F32),
        scratch_shapes=[pltpu.VMEM((tm, D_MODEL), MXU_DTYPE)],
        compiler_params=_params(("parallel", "arbitrary")),
        name="in_proj",
    )(x2d, gain, w_packed)


def _rope_tab_kernel(pos_ref, inva_ref, invi_ref, cosa_ref, sina_ref, cosi_ref, sini_ref):
    pos = pos_ref[...].astype(F32)
    lane = lax.broadcasted_iota(I32, (1, LANES), 1)

    def tables(inv, half, cos_ref, sin_ref):
        ang = pos * inv
        c = jnp.cos(ang)
        s = jnp.sin(ang)
        cos_ref[...] = jnp.where(lane < 2 * half, c, 1.0)
        sin_ref[...] = jnp.where(lane < half, -s, jnp.where(lane < 2 * half, s, 0.0))

    tables(inva_ref[...], ATTN_ROT_HALF, cosa_ref, sina_ref)
    tables(invi_ref[...], IDX_ROT_HALF, cosi_ref, sini_ref)


def _rope_tables(pos2d, inv_a, inv_i, tm):
    m = pos2d.shape[0]
    tab = jax.ShapeDtypeStruct((m, LANES), F32)
    row = pl.BlockSpec((tm, LANES), lambda i: (i, 0))
    one = pl.BlockSpec((1, LANES), lambda i: (0, 0))
    return pl.pallas_call(
        _rope_tab_kernel,
        grid=(m // tm,),
        in_specs=[pl.BlockSpec((tm, 1), lambda i: (i, 0)), one, one],
        out_specs=[row, row, row, row],
        out_shape=[tab, tab, tab, tab],
        compiler_params=_params(("parallel",)),
        name="rope_tables",
    )(pos2d, inv_a, inv_i)


def _rope(x, c, s, half):
    lane = lax.broadcasted_iota(I32, x.shape, 1)
    partner = jnp.where(lane < half, pltpu.roll(x, LANES - half, 1), pltpu.roll(x, half, 1))
    return x * c + partner * s


def _attn_prep_kernel(qa_ref, ka_ref, va_ref, qi_ref, ki_ref, misc_ref,
                      cosa_ref, sina_ref, cosi_ref, sini_ref, kgain_ref,
                      q_out, k_out, v_out, qi_out, ki_out, wi_out):
    ca, sa = cosa_ref[...], sina_ref[...]
    ci, si = cosi_ref[...], sini_ref[...]
    for h in range(ATTN_HEADS):
        sl = slice(h * LANES, (h + 1) * LANES)
        q_out[:, sl] = (_rope(qa_ref[:, sl], ca, sa, ATTN_ROT_HALF) * QK_SCALE_LOG2).astype(q_out.dtype)
    ones = jnp.ones((va_ref.shape[0], LANES), v_out.dtype)
    for h in range(ATTN_KV_HEADS):
        sl = slice(h * LANES, (h + 1) * LANES)
        k_out[:, sl] = _rope(ka_ref[:, sl], ca, sa, ATTN_ROT_HALF).astype(k_out.dtype)
        v_out[:, 2 * h * LANES:(2 * h + 1) * LANES] = va_ref[:, sl].astype(v_out.dtype)
        v_out[:, (2 * h + 1) * LANES:(2 * h + 2) * LANES] = ones
    for h in range(IDX_HEADS):
        sl = slice(h * LANES, (h + 1) * LANES)
        qi_out[:, sl] = _rope(qi_ref[:, sl], ci, si, IDX_ROT_HALF).astype(qi_out.dtype)
    ki = ki_ref[...]
    ms = jnp.sum(ki * ki, axis=-1, keepdims=True) * (1.0 / IDX_DIM)
    ki = ki * lax.rsqrt(ms + NORM_EPS) * kgain_ref[...]
    ki_out[...] = _rope(ki, ci, si, IDX_ROT_HALF).astype(ki_out.dtype)
    wi_out[...] = misc_ref[...] * (IDX_HEADS ** -0.5 * IDX_DIM ** -0.5)


def _attn_prep(p, tabs, kgain, tm):
    m = p.shape[0]
    cosa, sina, cosi, sini = tabs

    def col(width, off):
        blk = off // width
        return pl.BlockSpec((tm, width), lambda i: (i, blk))

    row = pl.BlockSpec((tm, LANES), lambda i: (i, 0))
    outs = [(ATTN_HEADS * LANES, MXU_DTYPE), (ATTN_KV_HEADS * LANES, MXU_DTYPE),
            (2 * ATTN_KV_HEADS * LANES, MXU_DTYPE), (IDX_HEADS * LANES, MXU_DTYPE),
            (LANES, MXU_DTYPE), (LANES, F32)]
    return pl.pallas_call(
        _attn_prep_kernel,
        grid=(m // tm,),
        in_specs=[col(1024, COL_QA), col(256, COL_KA), col(256, COL_VA), col(1024, COL_QI),
                  col(128, COL_KI), col(128, COL_MISC), row, row, row, row,
                  pl.BlockSpec((1, LANES), lambda i: (0, 0))],
        out_specs=[pl.BlockSpec((tm, w), lambda i: (i, 0)) for w, _ in outs],
        out_shape=[jax.ShapeDtypeStruct((m, w), d) for w, d in outs],
        compiler_params=_params(("parallel",)),
        name="attn_prep",
    )(p, p, p, p, p, p, cosa, sina, cosi, sini, kgain)


def _nt_dot(a, b):
    return lax.dot_general(a, b, (((1,), (1,)), ((), ())), preferred_element_type=F32)


def _dsa_kernel(q_ref, qi_ref, wi_ref, k_ref, v_ref, kidx_ref, u_ref, o_ref,
                keys_ref, m_ref, acc_ref, *, tq, tk, topk):
    i = pl.program_id(1)
    nkt = (i * tq + tq + tk - 1) // tk
    row_t = i * tq + lax.broadcasted_iota(I32, (tq, 1), 0)

    qi = qi_ref[...]
    wi = wi_ref[...]

    wb = [jnp.tile(jnp.broadcast_to(wi[:, MISC_WI + h:MISC_WI + h + 1], (tq, LANES)),
                   (1, tk // LANES)) for h in range(IDX_HEADS)]

    def index_tiles(kts):
        offs = [pl.multiple_of(kt * tk, tk) for kt in kts]
        raws = [[_nt_dot(qi[:, h * LANES:(h + 1) * LANES], kidx_ref[pl.ds(off, tk), :])
                 for h in range(IDX_HEADS)] for off in offs]
        for kt, off, raw in zip(kts, offs, raws):
            acc = jnp.maximum(raw[0], 0.0) * wb[0]
            for h in range(1, IDX_HEADS):
                acc = acc + jnp.maximum(raw[h], 0.0) * wb[h]
            bits = pltpu.bitcast(acc, I32)
            key = jnp.where(bits < 0, (bits ^ jnp.int32(0x7FFFFFFF)) + 1, bits)
            col = off + lax.broadcasted_iota(I32, (1, tk), 1)
            keys_ref[kt] = jnp.where(col <= row_t, key, jnp.int32(KEY_MIN))

    def idx_pair(j, carry):
        index_tiles([2 * j, 2 * j + 1])
        return carry

    lax.fori_loop(0, nkt // 2, idx_pair, 0)

    @pl.when(nkt % 2 == 1)
    def _():
        index_tiles([nkt - 1])

    def count_ge(cand):
        parts = []
        for r0 in range(0, tq, COUNT_ROWS):
            candb = jnp.broadcast_to(cand[r0:r0 + COUNT_ROWS], (COUNT_ROWS, LANES))

            def one(kt, acc, r0=r0, candb=candb):
                for c in range(tk // LANES):
                    kk = keys_ref[kt, r0:r0 + COUNT_ROWS, c * LANES:(c + 1) * LANES]
                    acc = acc + jnp.where(kk >= candb, 1.0, 0.0)
                return acc

            def two(j, acc, one=one):
                return one(2 * j + 1, one(2 * j, acc))

            acc = lax.fori_loop(0, nkt // 2, two, jnp.zeros((COUNT_ROWS, LANES), F32))
            acc = lax.cond(nkt % 2 == 1, functools.partial(one, nkt - 1), lambda a: a, acc)
            parts.append(jnp.sum(acc, axis=1, keepdims=True))
        return jnp.concatenate(parts, axis=0)

    kf = float(topk)

    def bit_step(b, ans, cnt):
        cand = ans + lax.shift_left(jnp.int32(1), 31 - b)
        cc = count_ge(cand)
        take = cc >= kf
        return jnp.where(take, cand, ans), jnp.where(take, cc, cnt)

    def bits_cond(c):
        b, _, cnt = c
        return (b < 32) & (jnp.max(jnp.abs(cnt - kf)) > 0.0)

    def bits_body(c):
        b, ans, cnt = c
        for u in range(BITS_PER_CHECK):
            ans, cnt = bit_step(b + u, ans, cnt)
        return b + BITS_PER_CHECK, ans, cnt

    thr, cnt = lax.fori_loop(
        0, UNCHECKED_BITS, lambda b, c: bit_step(b, *c),
        (jnp.full((tq, 1), KEY_MIN, I32), jnp.full((tq, 1), nkt * tk, I32).astype(F32)))
    _, thr, cnt = lax.while_loop(bits_cond, bits_body, (jnp.int32(UNCHECKED_BITS), thr, cnt))
    short = thr == jnp.int32(KEY_MIN)
    has_tie = jnp.max(jnp.where(short, 0.0, cnt - kf)) > 0.0

    q = q_ref[...]
    qg = [jnp.concatenate([q[:, (g * ATTN_GROUP + hh) * LANES:(g * ATTN_GROUP + hh + 1) * LANES]
                           for hh in range(ATTN_GROUP)], axis=0) for g in range(ATTN_KV_HEADS)]
    m_ref[...] = jnp.full(m_ref.shape, NEG_BIG, F32)
    acc_ref[...] = jnp.zeros(acc_ref.shape, F32)
    lane_tiles = tk // LANES

    def attend(tiles):
        units = [(pl.multiple_of(kt * tk, tk), bias, g, part) for kt, bias in tiles
                 for g in range(ATTN_KV_HEADS) for part in range(ATTN_GROUP // UNIT_HEADS)]

        def qk(unit):
            off, _, g, part = unit
            rows = slice(part * UNIT_HEADS * tq, (part + 1) * UNIT_HEADS * tq)
            return _nt_dot(qg[g][rows], k_ref[pl.ds(off, tk), g * LANES:(g + 1) * LANES])

        ahead = 2 * ATTN_GROUP // UNIT_HEADS
        scores = [qk(u) for u in units[:ahead]]
        for i, (off, bias, g, part) in enumerate(units):
            ps, alphas = [], []
            for hh in range(UNIT_HEADS):
                h = g * ATTN_GROUP + part * UNIT_HEADS + hh
                s = scores[i][hh * tq:(hh + 1) * tq] + bias
                m_prev = m_ref[h]
                m_new = jnp.maximum(m_prev, jnp.max(s, axis=1, keepdims=True))
                m_ref[h] = m_new
                alphas.append(jnp.exp2(m_prev - m_new))
                ps.append(jnp.exp2(s - jnp.tile(m_new, (1, lane_tiles))).astype(MXU_DTYPE))
            if i + ahead < len(units):
                scores.append(qk(units[i + ahead]))
            v_t = v_ref[pl.ds(off, tk), 2 * g * LANES:(2 * g + 2) * LANES]
            pv = jnp.dot(jnp.concatenate(ps, axis=0), v_t, preferred_element_type=F32)
            for hh in range(UNIT_HEADS):
                h = g * ATTN_GROUP + part * UNIT_HEADS + hh
                acc_ref[h] = (jnp.tile(alphas[hh], (1, 2)) * acc_ref[h]
                              + pv[hh * tq:(hh + 1) * tq])

    thr_floor = jnp.maximum(thr, jnp.int32(KEY_MIN + 1))

    def plain_bias(kt):
        return jnp.where(keys_ref[kt] >= thr_floor, 0.0, NEG_BIG)

    def pair_body(j, carry):
        attend([(2 * j, plain_bias(2 * j)), (2 * j + 1, plain_bias(2 * j + 1))])
        return carry

    def run_plain():
        lax.fori_loop(0, nkt // 2, pair_body, 0)

        @pl.when(nkt % 2 == 1)
        def _():
            attend([(nkt - 1, plain_bias(nkt - 1))])

    def run_ties():
        need = jnp.where(short, 0.0, kf - count_ge(thr + 1))

        def tie_body(kt, tie_carry):
            kk = keys_ref[kt]
            eq = kk == thr
            prefix = jnp.dot(jnp.where(eq, 1.0, 0.0).astype(BF16), u_ref[...],
                             preferred_element_type=F32)
            rank_ok = (prefix + tie_carry) <= need
            attend([(kt, jnp.where(kk > thr, 0.0,
                                   jnp.where(eq, jnp.where(rank_ok, 0.0, NEG_BIG), NEG_BIG)))])
            return tie_carry + prefix[:, tk - 1:tk]

        lax.fori_loop(0, nkt, tie_body, jnp.zeros((tq, 1), F32))

    lax.cond(has_tie, run_ties, run_plain)

    for h in range(ATTN_HEADS):
        acc = acc_ref[h]
        o_ref[:, h * LANES:(h + 1) * LANES] = (acc[:, :LANES] / acc[:, LANES:]).astype(o_ref.dtype)


def _dsa(q, k, v, qi, ki, wi, tie_u, tq, tk):
    b, s, _ = q.shape
    topk = min(INDEX_TOPK, s // 4)
    kern = functools.partial(_dsa_kernel, tq=tq, tk=tk, topk=topk)

    def qblk(w):
        return pl.BlockSpec((None, tq, w), lambda bi, i: (bi, i, 0))

    def full(w):
        return pl.BlockSpec((None, s, w), lambda bi, i: (bi, 0, 0))

    return pl.pallas_call(
        kern,
        grid=(b, s // tq),
        in_specs=[qblk(ATTN_HEADS * LANES), qblk(IDX_HEADS * LANES), qblk(LANES),
                  full(ATTN_KV_HEADS * LANES), full(2 * ATTN_KV_HEADS * LANES), full(LANES),
                  pl.BlockSpec((tk, tk), lambda bi, i: (0, 0))],
        out_specs=qblk(ATTN_HEADS * LANES),
        out_shape=jax.ShapeDtypeStruct((b, s, ATTN_HEADS * LANES), F32),
        scratch_shapes=[pltpu.VMEM((s // tk, tq, tk), I32),
                        pltpu.VMEM((ATTN_HEADS, tq, LANES), F32),
                        pltpu.VMEM((ATTN_HEADS, tq, 2 * LANES), F32)],
        compiler_params=_params(("parallel", "arbitrary")),
        name="dsa_attention",
    )(q, qi, wi, k, v, ki, tie_u)


def _gdn_prep_kernel(q_ref, k_ref, v_ref, qh_ref, kh_ref, vh_ref, misc_ref,
                     wq_ref, wk_ref, wv_ref, alog_ref, dtb_ref,
                     qn_out, kn_out, vv_out, gcol_out, buf_ref, *, tm, seq):
    seq_start = (pl.program_id(0) * tm) % seq == 0

    def conv_silu(x_ref, halo_ref, w_ref):
        x = x_ref[...]
        buf_ref[:SUBLANES, :] = jnp.where(seq_start, 0.0, halo_ref[...])
        buf_ref[SUBLANES:, :] = x
        w = w_ref[...]
        y = x * w[CONV_K - 1:CONV_K]
        for d in range(1, CONV_K):
            y = y + buf_ref[SUBLANES - d:SUBLANES - d + tm, :] * w[CONV_K - 1 - d:CONV_K - d]
        return y * jax.nn.sigmoid(y)

    def l2norm(x, out_ref, mult):
        for h in range(GDN_HEADS):
            sl = slice(h * LANES, (h + 1) * LANES)
            xh = x[:, sl]
            ss = jnp.sum(xh * xh, axis=-1, keepdims=True)
            out_ref[:, sl] = xh * (lax.rsqrt(ss + NORM_EPS) * mult)

    l2norm(conv_silu(q_ref, qh_ref, wq_ref), qn_out, GDN_DK ** -0.5)
    l2norm(conv_silu(k_ref, kh_ref, wk_ref), kn_out, 1.0)
    vv_out[...] = conv_silu(v_ref, vh_ref, wv_ref)

    misc = misc_ref[...]
    lane = lax.broadcasted_iota(I32, misc.shape, 1)
    row = lax.broadcasted_iota(I32, misc.shape, 0)
    is_a = (lane >= MISC_A) & (lane < MISC_A + GDN_HEADS)
    z = misc + dtb_ref[...]
    softplus = jnp.maximum(z, 0.0) + jnp.log(1.0 + jnp.exp(-jnp.abs(z)))
    g = jnp.where(is_a, -jnp.exp(alog_ref[...]) * softplus, 0.0)
    sh = 1
    while sh < CHUNK:
        g = g + jnp.where((row % CHUNK) >= sh, pltpu.roll(g, sh, 0), 0.0)
        sh *= 2
    is_beta = (lane >= MISC_BETA) & (lane < MISC_BETA + GDN_HEADS)
    gcol_out[...] = jnp.where(is_beta, jax.nn.sigmoid(misc), g)


def _gdn_prep(p, conv_w, alog_pad, dtb_pad, tm, seq):
    m = p.shape[0]
    wide = GDN_HEADS * LANES
    halo_blocks = tm // SUBLANES

    def col(off):
        blk = off // wide
        return pl.BlockSpec((tm, wide), lambda i: (i, blk))

    def halo(off):
        blk = off // wide
        return pl.BlockSpec((SUBLANES, wide), lambda i: (jnp.maximum(i * halo_blocks - 1, 0), blk))

    def wblk(j):
        return pl.BlockSpec((CONV_K, wide), lambda i: (0, j))

    one = pl.BlockSpec((1, LANES), lambda i: (0, 0))
    big = jax.ShapeDtypeStruct((m, wide), F32)
    kern = functools.partial(_gdn_prep_kernel, tm=tm, seq=seq)
    return pl.pallas_call(
        kern,
        grid=(m // tm,),
        in_specs=[col(COL_QB), col(COL_KB), col(COL_VB), halo(COL_QB), halo(COL_KB), halo(COL_VB),
                  pl.BlockSpec((tm, LANES), lambda i: (i, COL_MISC // LANES)),
                  wblk(0), wblk(1), wblk(2), one, one],
        out_specs=[pl.BlockSpec((tm, wide), lambda i: (i, 0))] * 3
        + [pl.BlockSpec((tm, LANES), lambda i: (i, 0))],
        out_shape=[big, big, big, jax.ShapeDtypeStruct((m, LANES), F32)],
        scratch_shapes=[pltpu.VMEM((tm + SUBLANES, wide), F32)],
        compiler_params=_params(("parallel",)),
        name="gdn_prep",
    )(p, p, p, p, p, p, p, conv_w, conv_w, conv_w, alog_pad, dtb_pad)


def _split(a):
    hi = a.astype(BF16)
    return hi, (a - hi.astype(F32)).astype(BF16)


def _dot3(a_split, b_split):
    (ah, al), (bh, bl) = a_split, b_split
    m = ah.shape[0]
    both = jnp.dot(jnp.concatenate([ah, al], axis=0), bh, preferred_element_type=F32)
    return both[:m] + both[m:] + jnp.dot(ah, bl, preferred_element_type=F32)


def _mxu_dot(a, b, dims=(((1,), (0,)), ((), ()))):
    return lax.dot_general(a.astype(MXU_DTYPE), b.astype(MXU_DTYPE), dims,
                           preferred_element_type=F32)


def _gdn_kernel(q_ref, k_ref, v_ref, gcol_ref, o_ref,
                state_ref, uw_st, qd_st, qk_st, kd_st, gt_st):
    h = pl.program_id(0)

    @pl.when(pl.program_id(1) == 0)
    def _():
        for ref in (state_ref, uw_st, qd_st, qk_st, kd_st, gt_st):
            ref[...] = jnp.zeros(ref.shape, F32)

    nb, rows = q_ref.shape[0], q_ref.shape[1]
    ii = lax.broadcasted_iota(I32, (CHUNK, CHUNK), 0)
    jj = lax.broadcasted_iota(I32, (CHUNK, CHUNK), 1)
    incl = ii >= jj
    strict = ii > jj
    eye = jnp.where(ii == jj, 1.0, 0.0)
    nt = (((1,), (1,)), ((), ()))
    tn = (((0,), (0,)), ((), ()))
    nsub = rows // CHUNK
    chains = [(b, sub) for b in range(nb) for sub in range(nsub)]

    states = [state_ref[b] for b in range(nb)]

    rec = {}

    def rec_read_state(sub):
        idx = [chains.index((b, sub)) for b in range(nb)]
        uws = [uw_st[c] for c in idx]
        rec["wss"] = [_mxu_dot(jnp.concatenate([uw[:, LANES:], qd_st[c]], axis=0), states[b])
                      for b, (c, uw) in enumerate(zip(idx, uws))]
        rec["v_news"] = [uw[:, :LANES] - ws[:CHUNK] for uw, ws in zip(uws, rec["wss"])]

    def rec_update(sub):
        sl = slice(sub * CHUNK, (sub + 1) * CHUNK)
        idx = [chains.index((b, sub)) for b in range(nb)]
        for b, c in enumerate(idx):
            o_ref[b, sl, :] = rec["wss"][b][CHUNK:] + _mxu_dot(qk_st[c], rec["v_news"][b])
        for b, c in enumerate(idx):
            states[b] = states[b] * gt_st[c][:1] + _mxu_dot(kd_st[c], rec["v_news"][b], tn)

    wide = GDN_PACK * CHUNK
    same_block = (lax.broadcasted_iota(I32, (wide, wide), 0) // CHUNK
                  == lax.broadcasted_iota(I32, (wide, wide), 1) // CHUNK)
    groups = [list(range(g, g + GDN_PACK)) for g in range(0, len(chains), GDN_PACK)]
    loc = {}

    def blockdiag(m):
        return jnp.where(same_block, jnp.tile(m, (GDN_PACK, 1)), jnp.zeros((), m.dtype))

    def packed_dot3(a_split, b_diag):
        (ah, al), (bh, bl) = a_split, b_diag
        both = jnp.dot(jnp.concatenate([ah, al], axis=0), bh, preferred_element_type=F32)
        return both[:CHUNK] + both[CHUNK:] + jnp.dot(ah, bl, preferred_element_type=F32)

    def local_start():
        col = {}
        for b in range(nb):
            gcol = gcol_ref[b]
            lane = lax.broadcasted_iota(I32, gcol.shape, 1)
            beta_all = jnp.sum(jnp.where(lane == MISC_BETA + h, gcol, 0.0), axis=1, keepdims=True)
            gc_all = jnp.sum(jnp.where(lane == MISC_A + h, gcol, 0.0), axis=1, keepdims=True)
            gc_sq = jnp.broadcast_to(gc_all, (rows, rows))
            col[b] = (beta_all, gc_all, gc_sq, gc_sq.T)
        xs, tinvs, rest = [], [], []
        for b, sub in chains:
            sl = slice(sub * CHUNK, (sub + 1) * CHUNK)
            beta_all, gc_all, gc_sq, gcr_sq = col[b]
            q, k, v = q_ref[b, sl, :], k_ref[b, sl, :], v_ref[b, sl, :]
            beta, gc = beta_all[sl], gc_all[sl]
            decay = jnp.where(incl, jnp.exp(gc_sq[sl, sl] - gcr_sq[sl, sl]), 0.0)
            kb = k * beta
            prods = _mxu_dot(jnp.concatenate([kb, q], axis=0), k, nt)
            x = -jnp.where(strict, prods[:CHUNK] * decay, 0.0)
            egc = jnp.exp(gc)
            g_last = gc[CHUNK - 1:CHUNK]
            xs.append(x)
            tinvs.append(eye + x)
            rest.append((jnp.concatenate([v * beta, kb * egc], axis=1), q * egc,
                         prods[CHUNK:] * decay, k * jnp.exp(g_last - gc), jnp.exp(g_last)))
        loc["rest"] = rest
        loc["xps"] = [_split(jnp.concatenate([xs[c] for c in grp], axis=1)) for grp in groups]
        loc["tps"] = [jnp.concatenate([tinvs[c] for c in grp], axis=1) for grp in groups]
        loc["diags"] = [(blockdiag(s[0]), blockdiag(s[1])) for s in loc["xps"]]

    def local_square():
        loc["xps"] = [_split(packed_dot3(s, d)) for s, d in zip(loc["xps"], loc["diags"])]
        loc["diags"] = [(blockdiag(s[0]), blockdiag(s[1])) for s in loc["xps"]]

    def local_accumulate():
        loc["tps"] = [t + packed_dot3(_split(t), d) for t, d in zip(loc["tps"], loc["diags"])]

    loc["uws"] = {}

    def local_solve(g):
        for j in range(GDN_PACK):
            c = groups[g][j]
            tinv = loc["tps"][g][:, j * CHUNK:(j + 1) * CHUNK]
            loc["uws"][c] = _dot3(_split(tinv), _split(loc["rest"][c][0]))

    def local_store():
        for c, r in enumerate(loc["rest"]):
            uw_st[c] = loc["uws"][c]
            qd_st[c] = r[1]
            qk_st[c] = r[2]
            kd_st[c] = r[3]
            gt_st[c] = jnp.broadcast_to(r[4], gt_st.shape[1:])

    levels = 0
    span = 1
    while 2 * span < CHUNK:
        levels += 1
        span *= 2
    local_stages = ([local_start] + [local_square, local_accumulate] * levels
                    + [functools.partial(local_solve, g) for g in range(len(groups))])
    rec_stages = [functools.partial(f, sub) for sub in range(nsub)
                  for f in (rec_read_state, rec_update)]
    for i in range(max(len(local_stages), len(rec_stages))):
        if i < len(local_stages):
            local_stages[i]()
        if i < len(rec_stages):
            rec_stages[i]()
    local_store()
    for b in range(nb):
        state_ref[b] = states[b]


def _gdn(qn, kn, vv, gcol, rows):
    batch, seq, _ = qn.shape
    nsteps = seq // rows
    nchains = batch * (rows // CHUNK)

    def cur(h, c):
        return (0, jnp.minimum(c, nsteps - 1), h)

    blk = pl.BlockSpec((batch, rows, LANES), cur)
    return pl.pallas_call(
        _gdn_kernel,
        grid=(GDN_HEADS, nsteps + 1),
        in_specs=[blk, blk, blk,
                  pl.BlockSpec((batch, rows, LANES), lambda h, c: (0, jnp.minimum(c, nsteps - 1), 0))],
        out_specs=pl.BlockSpec((batch, rows, LANES), lambda h, c: (0, jnp.maximum(c - 1, 0), h)),
        out_shape=jax.ShapeDtypeStruct(qn.shape, F32),
        scratch_shapes=[pltpu.VMEM((batch, GDN_DK, LANES), F32),
                        pltpu.VMEM((nchains, CHUNK, 2 * LANES), F32),
                        pltpu.VMEM((nchains, CHUNK, LANES), F32),
                        pltpu.VMEM((nchains, CHUNK, CHUNK), F32),
                        pltpu.VMEM((nchains, CHUNK, LANES), F32),
                        pltpu.VMEM((nchains, SUBLANES, LANES), F32)],
        compiler_params=_params(("parallel", "arbitrary")),
        name="gdn_chunks",
    )(qn, kn, vv, gcol)


def _merge_kernel(oa_ref, ob_ref, za_ref, zb_ref, ga_ref, gb_ref, x_ref, gng_ref, w_ref, fg_ref,
                  o_ref, mix_ref, *, final_norm):
    za = za_ref[...]
    ya = oa_ref[...] * (za * jax.nn.sigmoid(za))
    mix_ref[...] = (jax.nn.sigmoid(ga_ref[...]) * ya).astype(mix_ref.dtype)
    gng = gng_ref[...]
    for h in range(GDN_HEADS):
        sl = slice(h * LANES, (h + 1) * LANES)
        ob = ob_ref[:, sl]
        zb = zb_ref[:, sl]
        ms = jnp.mean(ob * ob, axis=-1, keepdims=True)
        yb = ob * lax.rsqrt(ms + NORM_EPS) * gng * (zb * jax.nn.sigmoid(zb))
        mix_ref[:, sl] += jax.nn.sigmoid(gb_ref[:, sl]) * yb
    y = x_ref[...] + jnp.dot(mix_ref[...].astype(MXU_DTYPE), w_ref[...], preferred_element_type=F32)
    if final_norm:
        ms = jnp.mean(y * y, axis=-1, keepdims=True)
        y = y * lax.rsqrt(ms + NORM_EPS) * fg_ref[...]
    o_ref[...] = y


def _merge_out(oa, ob, p, x2d, gdn_gain, w_out, final_gain, tm, final_norm):
    m = x2d.shape[0]

    def col(off):
        blk = off // D_MODEL
        return pl.BlockSpec((tm, D_MODEL), lambda i: (i, blk))

    row = pl.BlockSpec((tm, D_MODEL), lambda i: (i, 0))
    kern = functools.partial(_merge_kernel, final_norm=final_norm)
    return pl.pallas_call(
        kern,
        grid=(m // tm,),
        in_specs=[row, row, col(COL_ZA), col(COL_ZB), col(COL_GA), col(COL_GB), row,
                  pl.BlockSpec((1, LANES), lambda i: (0, 0)),
                  pl.BlockSpec((D_MODEL, D_MODEL), lambda i: (0, 0)),
                  pl.BlockSpec((1, D_MODEL), lambda i: (0, 0))],
        out_specs=row,
        out_shape=jax.ShapeDtypeStruct((m, D_MODEL), F32),
        scratch_shapes=[pltpu.VMEM((tm, D_MODEL), F32)],
        compiler_params=_params(("parallel",)),
        name="merge_out_proj",
    )(oa, ob, p, p, p, p, x2d, gdn_gain, w_out, final_gain)


def _pack_w_in(w):
    edges = [0]
    for n in IN_SPLITS:
        edges.append(edges[-1] + n)
    (qa, ka, va, za, qi, ki, wi, qkvb, zb, beta, a, gates) = [
        w[:, edges[j]:edges[j + 1]] for j in range(len(IN_SPLITS))]
    d = w.shape[0]
    qi = jnp.pad(qi.reshape(d, IDX_HEADS, IDX_DIM), ((0, 0), (0, 0), (0, LANES - IDX_DIM)))
    qi = qi.reshape(d, IDX_HEADS * LANES)
    ki = jnp.pad(ki, ((0, 0), (0, LANES - IDX_DIM)))
    misc = jnp.pad(jnp.concatenate([wi, beta, a], axis=1), ((0, 0), (0, LANES - 3 * 8)))
    packed = jnp.concatenate(
        [qa, za, zb, gates[:, :D_MODEL], gates[:, D_MODEL:], qkvb, qi, ka, va, ki, misc], axis=1)
    return packed.astype(MXU_DTYPE)


def _lane_pad(v, off):
    return jnp.pad(v.astype(F32), (off, LANES - off - v.shape[0])).reshape(1, LANES)


def _pick(n, prefs):
    for t in prefs:
        if n % t == 0:
            return t
    raise ValueError(f"no tile for {n}")


def kernel(x, positions, norm_gain, w_in, conv_w, a_log, dt_bias, gdn_norm_gain, idx_k_gain, w_out,
           final_gain):
    batch, seq, d = x.shape
    assert d == D_MODEL and seq % 128 == 0
    depth = w_in.shape[0]
    m = batch * seq
    tm_proj = _pick(m, (2048, 1024, 512, 256, 128))
    tn_proj = 768
    tm_prep = _pick(seq, (512, 256, 128))
    tm_merge = _pick(m, (256, 128))
    tq = 128
    tk = _pick(seq, (512, 256, 128))

    lane = jnp.arange(LANES)

    def inv_freq(half):
        rot = 2 * half
        f = ROPE_THETA ** (-jnp.arange(0, rot, 2, dtype=F32) / rot)
        return jnp.where(lane < rot, f[lane % half], 0.0).reshape(1, LANES).astype(F32)

    tabs = _rope_tables(positions.reshape(m, 1).astype(I32), inv_freq(ATTN_ROT_HALF),
                        inv_freq(IDX_ROT_HALF), tm_prep)
    tie_u = (jnp.arange(tk)[:, None] <= jnp.arange(tk)[None, :]).astype(BF16)

    x2d = x.reshape(m, d)
    for layer in range(depth):
        p = _project(x2d, norm_gain[layer].reshape(1, d), _pack_w_in(w_in[layer]), tm_proj, tn_proj)

        kgain = _lane_pad(idx_k_gain[layer], 0)
        q, k, v, qi, ki, wi = _attn_prep(p, tabs, kgain, tm_prep)
        shp = lambda t: t.reshape(batch, seq, t.shape[-1])
        o_a = _dsa(shp(q), shp(k), shp(v), shp(qi), shp(ki), shp(wi), tie_u, tq, tk).reshape(m, d)

        qn, kn, vv, gcol = _gdn_prep(p, conv_w[layer], _lane_pad(a_log[layer], MISC_A),
                                     _lane_pad(dt_bias[layer], MISC_A), tm_prep, seq)
        o_b = _gdn(shp(qn), shp(kn), shp(vv), shp(gcol), 8 * CHUNK).reshape(m, d)

        x2d = _merge_out(o_a, o_b, p, x2d, gdn_norm_gain[layer].reshape(1, LANES),
                         w_out[layer].astype(MXU_DTYPE), final_gain.reshape(1, d), tm_merge,
                         final_norm=(layer == depth - 1))
    return x2d.reshape(batch, seq, d)
```
